```python
import math
import jax, jax.numpy as jnp
from jax import lax
import numpy as np

D_MODEL = 1024
BATCH = 8
SEQ = 4096
DEPTH = 1

RWKV_WIDTH = D_MODEL // 2
RWKV_HEAD = 64
RWKV_HEADS = RWKV_WIDTH // RWKV_HEAD
DECAY_RANK = 64
AAA_RANK = 64
GATE_RANK = 128
RWKV_SPLITS = (RWKV_WIDTH, 2 * RWKV_WIDTH, 3 * RWKV_WIDTH,
               3 * RWKV_WIDTH + DECAY_RANK, 3 * RWKV_WIDTH + DECAY_RANK + AAA_RANK)
N_RWKV_COLS = 3 * RWKV_WIDTH + DECAY_RANK + AAA_RANK + GATE_RANK
LNX_EPS = 64e-5
S5_WIDTH = D_MODEL // 2
S5_GROUP = 16
S5_GROUPS = S5_WIDTH // S5_GROUP
S5_STATE = 64
STEP_MIN = 1e-3
STEP_MAX = 1e-1
N_IN_COLS = N_RWKV_COLS + S5_WIDTH + 2 * D_MODEL
D_FF = 256 * ((8 * D_MODEL // 3 + 255) // 256)
CONV_WIDTH = 3
NORM_EPS = 1e-6

kernel_name = 'hybrid_rwkv7_s5_gated_merge_convffn'


def rms_norm(x, g):
    xf = x.astype(jnp.float32)
    y = xf * lax.rsqrt(jnp.mean(xf * xf, axis=-1, keepdims=True) + NORM_EPS)
    return y.astype(x.dtype) * g


def token_shift(p, mu):
    prev = jnp.pad(p, ((0, 0), (1, 0), (0, 0)))[:, :-1]
    return p + (prev - p) * mu


def wkv7_scan(r, decay, k, v, a_vec, b_vec):
    bsz, _, nh, n = r.shape

    def step(S, inp):
        r_t, w_t, k_t, v_t, a_t, b_t = inp
        sa = jnp.einsum('bhij,bhj->bhi', S, a_t)
        S = (S * w_t[:, :, None, :] + sa[..., None] * b_t[:, :, None, :]
             + v_t[..., None] * k_t[:, :, None, :])
        return S, jnp.einsum('bhij,bhj->bhi', S, r_t)

    xs = tuple(jnp.moveaxis(t, 1, 0) for t in (r, decay, k, v, a_vec, b_vec))
    S0 = jnp.zeros((bsz, nh, n, n), jnp.float32)
    _, ys = lax.scan(step, S0, xs)
    return jnp.moveaxis(ys, 0, 1)


def rwkv7_branch(p, mu, w0, w2, a0, a2, g2, k_k, k_a, r_k, lnx_w, lnx_b):
    dtype = p.dtype
    f32 = jnp.float32
    bsz, L, _ = p.shape
    p = token_shift(p.astype(f32), mu.astype(f32))
    r, k, v, wd, ad, gd = jnp.split(p, RWKV_SPLITS, axis=-1)
    w = -jax.nn.softplus(-(w0 + jnp.tanh(wd) @ w2)) - 0.5
    decay = jnp.exp(-jnp.exp(w))
    a = jax.nn.sigmoid(a0 + ad @ a2)
    g = jax.nn.sigmoid(gd) @ g2
    hs = (bsz, L, RWKV_HEADS, RWKV_HEAD)
    kk = (k * k_k).reshape(hs)
    kk = kk / jnp.maximum(jnp.linalg.norm(kk, axis=-1, keepdims=True), 1e-12)
    k = k * (1.0 + (a - 1.0) * k_a)
    rh, kh, vh = r.reshape(hs), k.reshape(hs), v.reshape(hs)
    ah = a.reshape(hs)
    y = wkv7_scan(rh, decay.reshape(hs), kh, vh, -kk, kk * ah)
    mean = jnp.mean(y, axis=-1, keepdims=True)
    var = jnp.mean(jnp.square(y - mean), axis=-1, keepdims=True)
    y = ((y - mean) * lax.rsqrt(var + LNX_EPS)).reshape(bsz, L, RWKV_WIDTH) * lnx_w + lnx_b
    bonus = jnp.sum(rh * kh * r_k, axis=-1, keepdims=True) * vh
    y = (y + bonus.reshape(bsz, L, RWKV_WIDTH)) * g
    return y.astype(dtype)


def s5_combine(left, right):
    a_i, b_i = left
    a_j, b_j = right
    return a_j * a_i, a_j * b_i + b_j


def s5_branch(u, a_re, a_im, b_re, b_im, c_re, c_im, d, log_step, w_glu, b_glu):
    f32 = jnp.float32
    bsz, L, _ = u.shape
    lam = lax.complex(a_re.astype(f32), a_im.astype(f32))
    dt = jnp.exp(log_step.astype(f32))[:, None]
    a_bar = jnp.exp(lam * dt)
    b_bar = ((a_bar - 1.0) / lam)[..., None] * lax.complex(b_re.astype(f32), b_im.astype(f32))
    c = lax.complex(c_re.astype(f32), c_im.astype(f32))
    ug = u.astype(f32).reshape(bsz, L, S5_GROUPS, S5_GROUP)
    bu = jnp.einsum('gpc,blgc->blgp', b_bar, ug.astype(jnp.complex64))
    a_elems = jnp.broadcast_to(a_bar, (1, L, S5_GROUPS, S5_STATE))
    _, states = lax.associative_scan(s5_combine, (a_elems, bu), axis=1)
    y = jnp.real(jnp.einsum('gcp,blgp->blgc', c, states)) + d.astype(f32).reshape(S5_GROUPS, S5_GROUP) * ug
    y = jax.nn.gelu(y.reshape(bsz, L, S5_WIDTH)).astype(u.dtype)
    return y * jax.nn.sigmoid(y @ w_glu + b_glu)


def conv_ffn(h, w_up, conv_w, conv_b, w_down):
    L = h.shape[1]
    z = h @ w_up
    zp = jnp.pad(z, ((0, 0), (CONV_WIDTH - 1, 0), (0, 0)))
    z = conv_b + sum(conv_w[j] * zp[:, j:j + L] for j in range(CONV_WIDTH))
    gate, val = jnp.split(z, 2, axis=-1)
    return (jax.nn.gelu(gate) * val) @ w_down


def setup_inputs(seed: int = 0) -> dict:
    key = jax.random.key(seed)
    ks = jax.random.split(key, 40)
    f32 = jnp.float32
    Ld, W, G, P, C, F = DEPTH, RWKV_WIDTH, S5_GROUPS, S5_STATE, S5_GROUP, D_FF

    def nrm(k, shape, scale):
        return jax.random.normal(k, shape, f32) * scale

    def gain(k, shape):
        return 1.0 + 0.05 * jax.random.normal(k, shape, f32)

    n = jnp.arange(P, dtype=f32)
    return {
        'x': nrm(ks[0], (BATCH, SEQ, D_MODEL), 1.0),
        'norm_mix_pre': gain(ks[1], (Ld, D_MODEL)),
        'norm_mix_post': gain(ks[2], (Ld, D_MODEL)),
        'norm_ffn_pre': gain(ks[3], (Ld, D_MODEL)),
        'norm_ffn_post': gain(ks[4], (Ld, D_MODEL)),
        'w_in': nrm(ks[5], (Ld, D_MODEL, N_IN_COLS), D_MODEL ** -0.5),
        'b_gate': nrm(ks[6], (Ld, 2 * D_MODEL), 0.02),
        'rwkv_shift_mu': jax.random.uniform(ks[7], (Ld, N_RWKV_COLS), f32, 0.0, 1.0),
        'rwkv_w0': jax.random.uniform(ks[8], (Ld, W), f32, -6.0, -1.0),
        'rwkv_w2': nrm(ks[9], (Ld, DECAY_RANK, W), 0.1 * DECAY_RANK ** -0.5),
        'rwkv_a0': nrm(ks[10], (Ld, W), 0.1),
        'rwkv_a2': nrm(ks[11], (Ld, AAA_RANK, W), 0.1 * AAA_RANK ** -0.5),
        'rwkv_g2': nrm(ks[12], (Ld, GATE_RANK, W), GATE_RANK ** -0.5),
        'rwkv_k_k': 0.85 + 0.05 * jax.random.normal(ks[13], (Ld, W), f32),
        'rwkv_k_a': gain(ks[14], (Ld, W)),
        'rwkv_r_k': nrm(ks[15], (Ld, RWKV_HEADS, RWKV_HEAD), 0.3),
        'rwkv_lnx_w': gain(ks[16], (Ld, W)),
        'rwkv_lnx_b': nrm(ks[17], (Ld, W), 0.02),
        's5_a_re': -0.5 + 0.01 * jax.random.normal(ks[18], (Ld, G, P), f32),
        's5_a_im': math.pi * n + 0.01 * jax.random.normal(ks[19], (Ld, G, P), f32),
        's5_b_re': nrm(ks[20], (Ld, G, P, C), (2 * C) ** -0.5),
        's5_b_im': nrm(ks[21], (Ld, G, P, C), (2 * C) ** -0.5),
        's5_c_re': nrm(ks[22], (Ld, G, C, P), P ** -0.5),
        's5_c_im': nrm(ks[23], (Ld, G, C, P), P ** -0.5),
        's5_d': nrm(ks[24], (Ld, S5_WIDTH), 1.0),
        's5_log_step': jax.random.uniform(ks[25], (Ld, G), f32, math.log(STEP_MIN), math.log(STEP_MAX)),
        's5_w_glu': nrm(ks[26], (Ld, S5_WIDTH, S5_WIDTH), S5_WIDTH ** -0.5),
        's5_b_glu': nrm(ks[27], (Ld, S5_WIDTH), 0.02),
        'w_branch_rwkv': nrm(ks[28], (Ld, W, D_MODEL), W ** -0.5),
        'w_branch_s5': nrm(ks[29], (Ld, S5_WIDTH, D_MODEL), S5_WIDTH ** -0.5),
        'w_out': nrm(ks[30], (Ld, D_MODEL, D_MODEL), D_MODEL ** -0.5),
        'ffn_w_up': nrm(ks[31], (Ld, D_MODEL, 2 * F), D_MODEL ** -0.5),
        'ffn_conv_w': nrm(ks[32], (Ld, CONV_WIDTH, 2 * F), CONV_WIDTH ** -0.5),
        'ffn_conv_b': nrm(ks[33], (Ld, 2 * F), 0.02),
        'ffn_w_down': nrm(ks[34], (Ld, F, D_MODEL), F ** -0.5),
    }


def reference(x, norm_mix_pre, norm_mix_post, norm_ffn_pre, norm_ffn_post, w_in, b_gate,
              rwkv_shift_mu, rwkv_w0, rwkv_w2, rwkv_a0, rwkv_a2, rwkv_g2, rwkv_k_k, rwkv_k_a,
              rwkv_r_k, rwkv_lnx_w, rwkv_lnx_b, s5_a_re, s5_a_im, s5_b_re, s5_b_im, s5_c_re,
              s5_c_im, s5_d, s5_log_step, s5_w_glu, s5_b_glu, w_branch_rwkv, w_branch_s5, w_out,
              ffn_w_up, ffn_conv_w, ffn_conv_b, ffn_w_down):
    for l in range(DEPTH):
        h = rms_norm(x, norm_mix_pre[l])
        proj = h @ w_in[l]
        p_rwkv = proj[..., :N_RWKV_COLS]
        u_s5 = proj[..., N_RWKV_COLS:N_RWKV_COLS + S5_WIDTH]
        gates = jax.nn.sigmoid(proj[..., N_RWKV_COLS + S5_WIDTH:] + b_gate[l])
        g_rwkv, g_s5 = jnp.split(gates, 2, axis=-1)
        o_rwkv = rwkv7_branch(p_rwkv, rwkv_shift_mu[l], rwkv_w0[l], rwkv_w2[l], rwkv_a0[l],
                              rwkv_a2[l], rwkv_g2[l], rwkv_k_k[l], rwkv_k_a[l], rwkv_r_k[l],
                              rwkv_lnx_w[l], rwkv_lnx_b[l]) @ w_branch_rwkv[l]
        o_s5 = s5_branch(u_s5, s5_a_re[l], s5_a_im[l], s5_b_re[l], s5_b_im[l], s5_c_re[l],
                         s5_c_im[l], s5_d[l], s5_log_step[l], s5_w_glu[l], s5_b_glu[l]) @ w_branch_s5[l]
        mixed = (g_rwkv * o_rwkv + g_s5 * o_s5) @ w_out[l]
        x = x + rms_norm(mixed, norm_mix_post[l])
        h = rms_norm(x, norm_ffn_pre[l])
        f = conv_ffn(h, ffn_w_up[l], ffn_conv_w[l], ffn_conv_b[l], ffn_w_down[l])
        x = x + rms_norm(f, norm_ffn_post[l])
    return x
```

```python
import functools
import math

import jax
import jax.numpy as jnp
from jax import lax
from jax.experimental import pallas as pl
from jax.experimental.pallas import tpu as pltpu

F32 = jnp.float32
BF16 = jnp.bfloat16

NORM_EPS = 1e-6
LNX_EPS = 64e-5
HEAD = 64
CHUNK = 64
LANES = 128
SUBLANES = 8
S5_GROUP = 16
S5_STATE = 64
CONV_WIDTH = 3
VMEM_LIMIT = 56 * 1024 * 1024


def _rms(x, g):
    return x * lax.rsqrt(jnp.mean(x * x, axis=-1, keepdims=True) + NORM_EPS) * g


def _split(x):
    hi = x.astype(BF16)
    lo = (x - hi.astype(F32)).astype(BF16)
    return hi, lo


_NN = (((1,), (0,)), ((), ()))
_NT = (((1,), (1,)), ((), ()))
_TN = (((0,), (0,)), ((), ()))


def _mm(a, b, dims=_NN, passes=1):
    d = lambda p, q: lax.dot_general(p, q, dims, preferred_element_type=F32)
    if passes == 1:
        return d(a.astype(BF16), b.astype(BF16))
    ah, al = _split(a)
    if passes == 2:
        bh = b.astype(BF16)
        return d(ah, bh) + d(al, bh)
    bh, bl = _split(b)
    return d(ah, bh) + (d(ah, bl) + d(al, bh))


def _const_spec(shape):
    nd = len(shape)
    return pl.BlockSpec(shape, lambda *_: (0,) * nd)


def _params(sem):
    return pltpu.CompilerParams(dimension_semantics=sem, vmem_limit_bytes=VMEM_LIMIT)


def _in_proj_body(n_rwkv, n_s5, x_ref, g_ref, w_ref, bg_ref, p_ref, u_ref, gate_ref):
    h = _rms(x_ref[...], g_ref[...]).astype(BF16)
    p_ref[...] = jnp.dot(h, w_ref[:, :n_rwkv], preferred_element_type=F32)
    u_ref[...] = jnp.dot(h, w_ref[:, n_rwkv:n_rwkv + n_s5], preferred_element_type=F32)
    gate_ref[...] = jax.nn.sigmoid(
        jnp.dot(h, w_ref[:, n_rwkv + n_s5:], preferred_element_type=F32) + bg_ref[...])


def _in_proj(x2, g, w_in, b_gate, n_rwkv, n_s5, tm):
    t, d = x2.shape
    n_in = w_in.shape[1]
    n_gate = n_in - n_rwkv - n_s5
    row = lambda i: (i, 0)
    return pl.pallas_call(
        functools.partial(_in_proj_body, n_rwkv, n_s5),
        grid=(t // tm,),
        in_specs=[pl.BlockSpec((tm, d), row), _const_spec((1, d)), _const_spec((d, n_in)),
                  _const_spec((1, n_gate))],
        out_specs=[pl.BlockSpec((tm, n_rwkv), row), pl.BlockSpec((tm, n_s5), row),
                   pl.BlockSpec((tm, n_gate), row)],
        out_shape=[jax.ShapeDtypeStruct((t, n_rwkv), F32), jax.ShapeDtypeStruct((t, n_s5), F32),
                   jax.ShapeDtypeStruct((t, n_gate), F32)],
        compiler_params=_params(("arbitrary",)),
        name="in_proj",
    )(x2, g, w_in, b_gate)


def _mstack(x, m0):
    return jnp.concatenate([jnp.where(m0, x, 0.0), jnp.where(m0, 0.0, x)], axis=0)


def _wkv_chunk_pair(rt, at, kt, bt, bh, kh, v, pc, hbd, masks, passes):
    m0, strict, incl, eye, bd, diag = masks
    mm = functools.partial(_mm, passes=passes)
    ms = lambda x: _mstack(x, m0)
    c = CHUNK
    g = mm(jnp.concatenate([at, rt], axis=0), jnp.concatenate([ms(bt), ms(kt)], axis=0), _NT)
    a_ab = jnp.where(strict, g[:c, :LANES], 0.0)
    a_ak = jnp.where(strict, g[:c, LANES:], 0.0)
    a_rb = jnp.where(incl, g[c:, :LANES], 0.0)
    a_rk = jnp.where(incl, g[c:, LANES:], 0.0)
    q = mm(a_ab, ms(a_ab))
    t_inv = eye + a_ab
    n_sq = int(math.log2(c)) - 1
    for i in range(n_sq):
        if i < n_sq - 1:
            tq = mm(jnp.concatenate([t_inv, q], axis=0), ms(q))
            t_inv = t_inv + tq[:c]
            q = tq[c:]
        else:
            t_inv = t_inv + mm(t_inv, ms(q))
    w0 = mm(a_ak, ms(v))
    x1 = mm(t_inv, jnp.concatenate([ms(at), ms(w0)], axis=1))
    ap, u0 = x1[:, :LANES], x1[:, LANES:]
    zero = jnp.zeros((2 * c, LANES), F32)
    ry = mm(jnp.concatenate([a_rb, a_rk], axis=1),
            jnp.concatenate([jnp.concatenate([ms(ap), ms(u0)], axis=1),
                             jnp.concatenate([zero, ms(v)], axis=1)], axis=0))
    rp = rt + ry[:, :LANES]
    y0 = ry[:, LANES:]
    mh = mm(jnp.concatenate([bh, kh], axis=0),
            jnp.concatenate([jnp.concatenate([ap, u0], axis=1),
                             jnp.concatenate([zero[:c], v], axis=1)], axis=0), _TN)
    m = jnp.where(bd, mh[:, :LANES], 0.0) + jnp.where(diag, pc, 0.0)
    hadd = jnp.where(bd, mh[:, LANES:], 0.0)
    yh = mm(jnp.concatenate([rp, m], axis=0), hbd)
    return yh[:c] + y0, yh[c:] + hadd


def _rwkv_body(tl, width, passes, p_ref, mu_ref, w0_ref, wwa_ref, a0_ref, g2_ref, kk_ref, ka_ref, rk_ref,
               lnw_ref, lnb_ref, ones_ref, y_ref, carry_ref, h_ref, pre_ref, yacc_ref):
    li = pl.program_id(1)

    @pl.when(li == 0)
    def _():
        carry_ref[...] = jnp.zeros_like(carry_ref)
        h_ref[...] = jnp.zeros_like(h_ref)

    w3 = 3 * width
    p = p_ref[0]
    row = lax.broadcasted_iota(jnp.int32, (tl, 1), 0)
    prev = jnp.where(row == 0, carry_ref[...], pltpu.roll(p, 1, 0))
    carry_ref[...] = p[tl - 1:tl, :]
    ps = p + (prev - p) * mu_ref[...]
    r = ps[:, :width]
    k = ps[:, width:2 * width]
    v = ps[:, 2 * width:w3]
    wa = ps[:, w3:w3 + LANES]
    gd = ps[:, w3 + LANES:]
    lane = lax.broadcasted_iota(jnp.int32, (1, LANES), 1)
    m0 = lane < HEAD
    twa = _mm(jnp.where(m0, jnp.tanh(wa), wa), wwa_ref[...])
    w = -jax.nn.softplus(-(w0_ref[...] + twa[:, :width])) - 0.5
    lw = -jnp.exp(w)
    a = jax.nn.sigmoid(a0_ref[...] + twa[:, width:])
    g = _mm(jax.nn.sigmoid(gd), g2_ref[...])
    ones = ones_ref[...]
    segsum = lambda x: _mm(x, ones, passes=2)
    kk = k * kk_ref[...]
    kk = kk / jnp.maximum(jnp.sqrt(segsum(kk * kk)), 1e-12)
    k2 = k * (1.0 + (a - 1.0) * ka_ref[...])
    bonus = segsum(r * k2 * rk_ref[...]) * v
    pre_ref[0] = r
    pre_ref[1] = k2
    pre_ref[2] = v
    pre_ref[3] = -kk
    pre_ref[4] = kk * a
    pre_ref[5] = lw

    c = CHUNK
    rowc = lax.broadcasted_iota(jnp.int32, (c, LANES), 0)
    colc = lax.broadcasted_iota(jnp.int32, (c, LANES), 1) % HEAD
    r2 = lax.broadcasted_iota(jnp.int32, (LANES, LANES), 0)
    c2 = lax.broadcasted_iota(jnp.int32, (LANES, LANES), 1)
    masks = (m0, colc < rowc, colc <= rowc, (colc == rowc).astype(F32),
             (r2 // HEAD) == (c2 // HEAD), r2 == c2)
    ltri = (lax.broadcasted_iota(jnp.int32, (c, c), 1) <= lax.broadcasted_iota(jnp.int32, (c, c), 0)).astype(F32)
    n_pairs = width // LANES
    hs = [h_ref[j] for j in range(n_pairs)]
    for ci in range(tl // c):
        rows = slice(ci * c, (ci + 1) * c)
        lwc = pre_ref[5, rows, :]
        cum = _mm(ltri, lwc, passes=3)
        pw = jnp.exp(cum)
        pinv = jnp.exp(-cum)
        rt = pre_ref[0, rows, :] * pw
        at = pre_ref[3, rows, :] * jnp.exp(cum - lwc)
        kt = pre_ref[1, rows, :] * pinv
        bt = pre_ref[4, rows, :] * pinv
        pc = pw[c - 1:c, :]
        bh = bt * pc
        kh = kt * pc
        vc = pre_ref[2, rows, :]
        for j in range(n_pairs):
            ln = slice(j * LANES, (j + 1) * LANES)
            y, hs[j] = _wkv_chunk_pair(rt[:, ln], at[:, ln], kt[:, ln], bt[:, ln], bh[:, ln], kh[:, ln],
                                       vc[:, ln], pc[:, ln], hs[j], masks, passes)
            yacc_ref[rows, ln] = y
    for j in range(n_pairs):
        h_ref[j] = hs[j]

    y = yacc_ref[...]
    mean = segsum(y) * (1.0 / HEAD)
    yc = y - mean
    var = segsum(yc * yc) * (1.0 / HEAD)
    yn = yc * lax.rsqrt(var + LNX_EPS) * lnw_ref[...] + lnb_ref[...]
    y_ref[0] = (yn + bonus) * g


def _rwkv(p3, mu, w0, wwa, a0, g2, k_k, k_a, r_k, lnx_w, lnx_b, ones_bd, width, tl, passes):
    b, l, n_rwkv = p3.shape
    n_pairs = width // LANES
    consts = [mu, w0, wwa, a0, g2, k_k, k_a, r_k, lnx_w, lnx_b, ones_bd]
    return pl.pallas_call(
        functools.partial(_rwkv_body, tl, width, passes),
        grid=(b, l // tl),
        in_specs=[pl.BlockSpec((1, tl, n_rwkv), lambda i, j: (i, j, 0))] + [_const_spec(c.shape) for c in consts],
        out_specs=pl.BlockSpec((1, tl, width), lambda i, j: (i, j, 0)),
        out_shape=jax.ShapeDtypeStruct((b, l, width), F32),
        scratch_shapes=[pltpu.VMEM((1, n_rwkv), F32), pltpu.VMEM((n_pairs, LANES, LANES), F32),
                        pltpu.VMEM((6, tl, width), F32), pltpu.VMEM((tl, width), F32)],
        compiler_params=_params(("arbitrary", "arbitrary")),
        name="rwkv7",
    )(p3, *consts)


def _s5_disc_body(are_ref, aim_ref, dt_ref, bre_ref, bim_ref, abr_ref, abi_ref, bbr_ref, bbi_ref):
    lre, lim, dt = are_ref[...], aim_ref[...], jnp.exp(dt_ref[...])
    mag = jnp.exp(lre * dt)
    abr = mag * jnp.cos(lim * dt)
    abi = mag * jnp.sin(lim * dt)
    inv = 1.0 / (lre * lre + lim * lim)
    nr, ni = abr - 1.0, abi
    cr = (nr * lre + ni * lim) * inv
    ci = (ni * lre - nr * lim) * inv
    bre, bim = bre_ref[...], bim_ref[...]
    abr_ref[...] = abr
    abi_ref[...] = abi
    bbr_ref[...] = cr * bre - ci * bim
    bbi_ref[...] = cr * bim + ci * bre


def _s5_disc(a_re, a_im, log_step, b_re, b_im):
    g, p, c = b_re.shape
    col = lambda x: x.reshape(g * p, 1)
    dt = jnp.broadcast_to(log_step[:, None], (g, p))
    outs = pl.pallas_call(
        _s5_disc_body,
        out_shape=[jax.ShapeDtypeStruct((g * p, 1), F32)] * 2 + [jax.ShapeDtypeStruct((g * p, c), F32)] * 2,
        name="s5_disc",
    )(col(a_re), col(a_im), col(dt), b_re.reshape(g * p, c), b_im.reshape(g * p, c))
    abr, abi, bbr, bbi = outs
    return abr.reshape(g, p), abi.reshape(g, p), bbr.reshape(g, p, c), bbi.reshape(g, p, c)


def _s5_body(tr, ns, col_w, u_ref, bblk_ref, are_ref, aim_ref, cblk_ref, d_ref, wglu_ref, bglu_ref,
             y_ref, st_ref, s_ref):
    @pl.when(pl.program_id(0) == 0)
    def _():
        s_ref[...] = jnp.zeros_like(s_ref)

    u = u_ref[...]
    st_ref[...] = jnp.dot(u.astype(BF16), bblk_ref[...], preferred_element_type=F32)
    nb = SUBLANES
    for cg in range(ns // col_w):
        re_c = slice(cg * col_w, (cg + 1) * col_w)
        im_c = slice(ns + cg * col_w, ns + (cg + 1) * col_w)
        are = jnp.broadcast_to(are_ref[:, re_c], (nb, col_w))
        aim = jnp.broadcast_to(aim_ref[:, re_c], (nb, col_w))

        def step(t, carry):
            sre, sim = carry
            rows = pl.ds(pl.multiple_of(t * nb, nb), nb)
            nre = are * sre - aim * sim + st_ref[rows, re_c]
            nim = are * sim + aim * sre + st_ref[rows, im_c]
            st_ref[rows, re_c] = nre
            st_ref[rows, im_c] = nim
            return nre, nim

        sre, sim = lax.fori_loop(0, tr // nb, step, (s_ref[:, re_c], s_ref[:, im_c]), unroll=4)
        s_ref[:, re_c] = sre
        s_ref[:, im_c] = sim
    y = jnp.dot(st_ref[...].astype(BF16), cblk_ref[...], preferred_element_type=F32) + d_ref[...] * u
    y = jax.nn.gelu(y)
    y_ref[...] = y * jax.nn.sigmoid(jnp.dot(y.astype(BF16), wglu_ref[...], preferred_element_type=F32)
                                    + bglu_ref[...])


def _s5(u_tm, bblk, are, aim, cblk, d, w_glu, b_glu, tr):
    t, width = u_tm.shape
    ns = are.shape[1]
    consts = [bblk, are, aim, cblk, d, w_glu, b_glu]
    row = lambda i: (i, 0)
    return pl.pallas_call(
        functools.partial(_s5_body, tr, ns, 4 * LANES),
        grid=(t // tr,),
        in_specs=[pl.BlockSpec((tr, width), row)] + [_const_spec(c.shape) for c in consts],
        out_specs=pl.BlockSpec((tr, width), row),
        out_shape=jax.ShapeDtypeStruct((t, width), F32),
        scratch_shapes=[pltpu.VMEM((tr, 2 * ns), F32), pltpu.VMEM((SUBLANES, 2 * ns), F32)],
        compiler_params=_params(("arbitrary",)),
        name="s5",
    )(u_tm, *consts)


def _merge_body(d, x_ref, yr_ref, ys_ref, gate_ref, wbr_ref, wbs_ref, wout_ref, g_ref, o_ref):
    o_r = jnp.dot(yr_ref[...].astype(BF16), wbr_ref[...], preferred_element_type=F32)
    o_s = jnp.dot(ys_ref[...].astype(BF16), wbs_ref[...], preferred_element_type=F32)
    mixed = gate_ref[:, :d] * o_r + gate_ref[:, d:] * o_s
    mixed = jnp.dot(mixed.astype(BF16), wout_ref[...], preferred_element_type=F32)
    o_ref[...] = x_ref[...] + _rms(mixed, g_ref[...])


def _merge(x2, yr, ys, gates, wbr, wbs, wout, g, tm):
    t, d = x2.shape
    row = lambda i: (i, 0)
    consts = [wbr, wbs, wout, g]
    return pl.pallas_call(
        functools.partial(_merge_body, d),
        grid=(t // tm,),
        in_specs=[pl.BlockSpec((tm, d), row), pl.BlockSpec((tm, yr.shape[1]), row),
                  pl.BlockSpec((tm, ys.shape[1]), row), pl.BlockSpec((tm, 2 * d), row)]
                 + [_const_spec(c.shape) for c in consts],
        out_specs=pl.BlockSpec((tm, d), row),
        out_shape=jax.ShapeDtypeStruct((t, d), F32),
        compiler_params=_params(("arbitrary",)),
        name="merge",
    )(x2, yr, ys, gates, *consts)


def _ffn_body(tm, f, cb, x_ref, gpre_ref, wup_ref, cw_ref, cbias_ref, wdown_ref, gpost_ref, o_ref,
              carry_ref, act_ref):
    @pl.when(pl.program_id(1) == 0)
    def _():
        carry_ref[...] = jnp.zeros_like(carry_ref)

    x = x_ref[0]
    h = _rms(x, gpre_ref[...]).astype(BF16)
    nb = SUBLANES
    sub = lax.broadcasted_iota(jnp.int32, (nb, 1), 0)

    def conv(cols):
        z = jnp.dot(h, wup_ref[:, cols], preferred_element_type=F32)
        tail = carry_ref[:, cols]
        carry_ref[:, cols] = z[tm - nb:, :]
        out = cbias_ref[:, cols] + cw_ref[CONV_WIDTH - 1:CONV_WIDTH, cols] * z
        for s in range(1, CONV_WIDTH):
            zr = pltpu.roll(z, s, 0)
            top = jnp.where(sub < s, pltpu.roll(tail, s, 0), zr[:nb])
            zs = jnp.concatenate([top, zr[nb:]], axis=0)
            out = out + cw_ref[CONV_WIDTH - 1 - s:CONV_WIDTH - s, cols] * zs
        return out

    for j in range(f // cb):
        gate = conv(slice(j * cb, (j + 1) * cb))
        val = conv(slice(f + j * cb, f + (j + 1) * cb))
        act_ref[:, j * cb:(j + 1) * cb] = (jax.nn.gelu(gate) * val).astype(BF16)
    y = jnp.dot(act_ref[...], wdown_ref[...], preferred_element_type=F32)
    o_ref[0] = x + _rms(y, gpost_ref[...])


def _ffn(x3, g_pre, w_up, conv_w, conv_b, w_down, g_post, tm):
    b, l, d = x3.shape
    f = w_down.shape[0]
    cb = 2 * LANES
    consts = [g_pre, w_up, conv_w, conv_b, w_down, g_post]
    return pl.pallas_call(
        functools.partial(_ffn_body, tm, f, cb),
        grid=(b, l // tm),
        in_specs=[pl.BlockSpec((1, tm, d), lambda i, j: (i, j, 0))] + [_const_spec(c.shape) for c in consts],
        out_specs=pl.BlockSpec((1, tm, d), lambda i, j: (i, j, 0)),
        out_shape=jax.ShapeDtypeStruct((b, l, d), F32),
        scratch_shapes=[pltpu.VMEM((SUBLANES, 2 * f), F32), pltpu.VMEM((tm, f), BF16)],
        compiler_params=_params(("arbitrary", "arbitrary")),
        name="conv_ffn",
    )(x3, *consts)


def _block_diag(blocks):
    g, r, c = blocks.shape
    eye = jnp.eye(g, dtype=blocks.dtype)
    return (blocks[:, :, None, :] * eye[:, None, :, None]).reshape(g * r, g * c)


def _layer(x, norm_mix_pre, norm_mix_post, norm_ffn_pre, norm_ffn_post, w_in, b_gate, mu, w0, w2, a0, a2, g2,
           k_k, k_a, r_k, lnx_w, lnx_b, s5_a_re, s5_a_im, s5_b_re, s5_b_im, s5_c_re, s5_c_im, s5_d, s5_log_step,
           s5_w_glu, s5_b_glu, w_branch_rwkv, w_branch_s5, w_out, ffn_w_up, ffn_conv_w, ffn_conv_b, ffn_w_down,
           tiles):
    b, l, d = x.shape
    t = b * l
    width = w0.shape[0]
    s5_width = s5_d.shape[0]
    n_rwkv = mu.shape[0]
    rank_w, rank_a = w2.shape[0], a2.shape[0]
    assert rank_w == HEAD and rank_a == HEAD and g2.shape[0] == LANES and b == SUBLANES
    rowv = lambda vec: vec.reshape(1, -1)
    x2 = x.reshape(t, d)

    p, u, gates = _in_proj(x2, rowv(norm_mix_pre), w_in.astype(BF16), rowv(b_gate), n_rwkv, s5_width,
                           tiles["in"])

    wwa = jnp.zeros((LANES, 2 * width), F32).at[:rank_w, :width].set(w2).at[rank_w:, width:].set(a2)
    ones_bd = _block_diag(jnp.ones((width // HEAD, HEAD, HEAD), F32)).astype(BF16)
    y_rwkv = _rwkv(p.reshape(b, l, n_rwkv), rowv(mu), rowv(w0), wwa.astype(BF16), rowv(a0), g2.astype(BF16),
                   rowv(k_k), rowv(k_a), rowv(r_k), rowv(lnx_w), rowv(lnx_b), ones_bd, width, tiles["rwkv"],
                   tiles["passes"]).reshape(t, width)

    abr, abi, bbr, bbi = _s5_disc(s5_a_re, s5_a_im, s5_log_step, s5_b_re, s5_b_im)
    bblk = jnp.concatenate([_block_diag(bbr.transpose(0, 2, 1)), _block_diag(bbi.transpose(0, 2, 1))], axis=1)
    cblk = jnp.concatenate([_block_diag(s5_c_re.transpose(0, 2, 1)), -_block_diag(s5_c_im.transpose(0, 2, 1))],
                           axis=0)
    u_tm = u.reshape(b, l, s5_width).transpose(1, 0, 2).reshape(t, s5_width)
    y_s5 = _s5(u_tm, bblk.astype(BF16), abr.reshape(1, -1), abi.reshape(1, -1), cblk.astype(BF16), rowv(s5_d),
               s5_w_glu.astype(BF16), rowv(s5_b_glu), tiles["s5"])
    y_s5 = y_s5.reshape(l, b, s5_width).transpose(1, 0, 2).reshape(t, s5_width)

    x1 = _merge(x2, y_rwkv, y_s5, gates, w_branch_rwkv.astype(BF16), w_branch_s5.astype(BF16),
                w_out.astype(BF16), rowv(norm_mix_post), tiles["merge"])
    out = _ffn(x1.reshape(b, l, d), rowv(norm_ffn_pre), ffn_w_up.astype(BF16), ffn_conv_w, rowv(ffn_conv_b),
               ffn_w_down.astype(BF16), rowv(norm_ffn_post), tiles["ffn"])
    return out


TILES = {"in": 256, "rwkv": 128, "s5": 256, "merge": 256, "ffn": 256, "passes": 3}


def kernel(x, norm_mix_pre, norm_mix_post, norm_ffn_pre, norm_ffn_post, w_in, b_gate, rwkv_shift_mu, rwkv_w0, rwkv_w2, rwkv_a0, rwkv_a2, rwkv_g2, rwkv_k_k, rwkv_k_a, rwkv_r_k, rwkv_lnx_w, rwkv_lnx_b, s5_a_re, s5_a_im, s5_b_re, s5_b_im, s5_c_re, s5_c_im, s5_d, s5_log_step, s5_w_glu, s5_b_glu, w_branch_rwkv, w_branch_s5, w_out, ffn_w_up, ffn_conv_w, ffn_conv_b, ffn_w_down):
    depth = w_in.shape[0]
    for i in range(depth):
        x = _layer(x, norm_mix_pre[i], norm_mix_post[i], norm_ffn_pre[i], norm_ffn_post[i], w_in[i], b_gate[i],
                   rwkv_shift_mu[i], rwkv_w0[i], rwkv_w2[i], rwkv_a0[i], rwkv_a2[i], rwkv_g2[i], rwkv_k_k[i],
                   rwkv_k_a[i], rwkv_r_k[i].reshape(-1), rwkv_lnx_w[i], rwkv_lnx_b[i], s5_a_re[i], s5_a_im[i],
                   s5_b_re[i], s5_b_im[i], s5_c_re[i], s5_c_im[i], s5_d[i], s5_log_step[i], s5_w_glu[i],
                   s5_b_glu[i], w_branch_rwkv[i], w_branch_s5[i], w_out[i], ffn_w_up[i], ffn_conv_w[i],
                   ffn_conv_b[i], ffn_w_down[i], TILES)
    return x
```

```python
import functools
import math

import jax
import jax.numpy as jnp
from jax import lax
from jax.experimental import pallas as pl
from jax.experimental.pallas import tpu as pltpu

F32 = jnp.float32
BF16 = jnp.bfloat16

NORM_EPS = 1e-6
LNX_EPS = 64e-5
HEAD = 64
CHUNK = 64
LANES = 128
SUBLANES = 8
S5_GROUP = 16
S5_STATE = 64
CONV_WIDTH = 3
VMEM_LIMIT = 56 * 1024 * 1024


def _rms(x, g):
    return x * lax.rsqrt(jnp.mean(x * x, axis=-1, keepdims=True) + NORM_EPS) * g


def _split(x):
    hi = x.astype(BF16)
    lo = (x - hi.astype(F32)).astype(BF16)
    return hi, lo


_NN = (((1,), (0,)), ((), ()))
_NT = (((1,), (1,)), ((), ()))
_TN = (((0,), (0,)), ((), ()))


def _mm(a, b, dims=_NN, passes=1):
    d = lambda p, q: lax.dot_general(p, q, dims, preferred_element_type=F32)
    if passes == 1:
        return d(a.astype(BF16), b.astype(BF16))
    ah, al = _split(a)
    if passes == 2:
        bh = b.astype(BF16)
        return d(ah, bh) + d(al, bh)
    bh, bl = _split(b)
    return d(ah, bh) + (d(ah, bl) + d(al, bh))


def _const_spec(shape):
    nd = len(shape)
    return pl.BlockSpec(shape, lambda *_: (0,) * nd)


def _params(sem):
    return pltpu.CompilerParams(dimension_semantics=sem, vmem_limit_bytes=VMEM_LIMIT)


def _in_proj_body(n_rwkv, n_s5, x_ref, g_ref, w_ref, bg_ref, p_ref, u_ref, gate_ref):
    h = _rms(x_ref[...], g_ref[...]).astype(BF16)
    p_ref[...] = jnp.dot(h, w_ref[:, :n_rwkv], preferred_element_type=F32)
    u_ref[...] = jnp.dot(h, w_ref[:, n_rwkv:n_rwkv + n_s5], preferred_element_type=F32)
    gate_ref[...] = jax.nn.sigmoid(
        jnp.dot(h, w_ref[:, n_rwkv + n_s5:], preferred_element_type=F32) + bg_ref[...])


def _in_proj(x2, g, w_in, b_gate, n_rwkv, n_s5, tm):
    t, d = x2.shape
    n_in = w_in.shape[1]
    n_gate = n_in - n_rwkv - n_s5
    row = lambda i: (i, 0)
    return pl.pallas_call(
        functools.partial(_in_proj_body, n_rwkv, n_s5),
        grid=(t // tm,),
        in_specs=[pl.BlockSpec((tm, d), row), _const_spec((1, d)), _const_spec((d, n_in)),
                  _const_spec((1, n_gate))],
        out_specs=[pl.BlockSpec((tm, n_rwkv), row), pl.BlockSpec((tm, n_s5), row),
                   pl.BlockSpec((tm, n_gate), row)],
        out_shape=[jax.ShapeDtypeStruct((t, n_rwkv), F32), jax.ShapeDtypeStruct((t, n_s5), F32),
                   jax.ShapeDtypeStruct((t, n_gate), F32)],
        compiler_params=_params(("arbitrary",)),
        name="in_proj",
    )(x2, g, w_in, b_gate)


def _mstack(x, m0):
    z = jnp.zeros_like(x)
    return jnp.concatenate([jnp.where(m0, x, z), jnp.where(m0, z, x)], axis=0)


def _dot(a, b, dims=_NN):
    return lax.dot_general(a, b, dims, preferred_element_type=F32)


def _cat0(*xs):
    return jnp.concatenate(xs, axis=0)


def _cat1(*xs):
    return jnp.concatenate(xs, axis=1)


def _wkv_chunks(rt, at, kt, bt, bh, kh, v, pc, hs, masks):
    m0, strict, incl, eye, bd, diag = masks
    c = CHUNK
    ms = lambda x: _mstack(x, m0)
    b16 = lambda x: x.astype(BF16)
    ids = [(ci, j) for ci in range(len(rt)) for j in range(len(rt[0]))]
    pick = lambda nested: [nested[ci][j] for ci, j in ids]
    rt32 = pick(rt)
    rt, at, kt, bt, bh, kh, v = [[b16(x) for x in pick(n)] for n in (rt, at, kt, bt, bh, kh, v)]
    n = range(len(ids))
    msv = [ms(x) for x in v]
    g = [_dot(_cat0(at[i], rt[i]), _cat0(ms(bt[i]), ms(kt[i])), _NT) for i in n]
    a_ab = [jnp.where(strict, x[:c, :LANES], 0.0) for x in g]
    a_ak = [b16(jnp.where(strict, x[:c, LANES:], 0.0)) for x in g]
    a_rbk = [b16(_cat1(jnp.where(incl, x[c:, :LANES], 0.0), jnp.where(incl, x[c:, LANES:], 0.0))) for x in g]
    ab16 = [b16(x) for x in a_ab]
    q = [b16(_dot(ab16[i], ms(ab16[i]))) for i in n]
    t_inv = [eye + x for x in a_ab]
    n_sq = int(math.log2(c)) - 1
    for s in range(n_sq - 1):
        tq = [_dot(_cat0(b16(t_inv[i]), q[i]), ms(q[i])) for i in n]
        t_inv = [t_inv[i] + tq[i][:c] for i in n]
        q = [b16(tq[i][c:]) for i in n]
    t16 = [b16(t_inv[i]) for i in n]
    t_inv = [b16(t_inv[i] + _dot(t16[i], ms(q[i]))) for i in n]
    w0 = [b16(_dot(a_ak[i], msv[i])) for i in n]
    x1 = [_dot(t_inv[i], _cat1(ms(at[i]), ms(w0[i]))) for i in n]
    ap = [b16(x[:, :LANES]) for x in x1]
    u0 = [b16(x[:, LANES:]) for x in x1]
    zero = jnp.zeros((2 * c, LANES), BF16)
    ry = [_dot(a_rbk[i], _cat0(_cat1(ms(ap[i]), ms(u0[i])), _cat1(zero, msv[i]))) for i in n]
    mh = [_dot(_cat0(bh[i], kh[i]), _cat0(_cat1(ap[i], u0[i]), _cat1(zero[:c], v[i])), _TN) for i in n]
    rp = [b16(rt32[i] + ry[i][:, :LANES]) for i in n]
    ys, hs = [], list(hs)
    for i, (ci, j) in enumerate(ids):
        m = jnp.where(bd, mh[i][:, :LANES], 0.0) + jnp.where(diag, pc[ci][j], 0.0)
        yh = _dot(_cat0(rp[i], b16(m)), b16(hs[j]))
        ys.append(yh[:c] + ry[i][:, LANES:])
        hs[j] = yh[c:] + jnp.where(bd, mh[i][:, LANES:], 0.0)
    n_pairs = len(hs)
    return [ys[ci * n_pairs:(ci + 1) * n_pairs] for ci in range(len(ids) // n_pairs)], hs


def _split3(x):
    hi = x.astype(BF16)
    r1 = x - hi.astype(F32)
    mid = r1.astype(BF16)
    return hi, mid, (r1 - mid.astype(F32)).astype(BF16)


def _rwkv_body(tl, width, p_ref, mu_ref, w0_ref, wwa_ref, a0_ref, g2_ref, kk_ref, ka_ref, rk_ref,
               lnw_ref, lnb_ref, ones_ref, y_ref, carry_ref, h_ref, pre_ref, yacc_ref):
    li = pl.program_id(1)

    @pl.when(li == 0)
    def _():
        carry_ref[...] = jnp.zeros_like(carry_ref)
        h_ref[...] = jnp.zeros_like(h_ref)

    w3 = 3 * width
    p = p_ref[0]
    row = lax.broadcasted_iota(jnp.int32, (tl, 1), 0)
    prev = jnp.where(row == 0, carry_ref[...], pltpu.roll(p, 1, 0))
    carry_ref[...] = p[tl - 1:tl, :]
    ps = p + (prev - p) * mu_ref[...]
    r = ps[:, :width]
    k = ps[:, width:2 * width]
    v = ps[:, 2 * width:w3]
    wa = ps[:, w3:w3 + LANES]
    gd = ps[:, w3 + LANES:]
    lane = lax.broadcasted_iota(jnp.int32, (1, LANES), 1)
    m0 = lane < HEAD
    twa = _mm(jnp.where(m0, jnp.tanh(wa), wa), wwa_ref[...])
    w = -jax.nn.softplus(-(w0_ref[...] + twa[:, :width])) - 0.5
    lw = -jnp.exp(w)
    a = jax.nn.sigmoid(a0_ref[...] + twa[:, width:])
    g = _mm(jax.nn.sigmoid(gd), g2_ref[...])
    ones = ones_ref[...]
    segsum = lambda x: _mm(x, ones, passes=2)
    kk = k * kk_ref[...]
    kk = kk / jnp.maximum(jnp.sqrt(segsum(kk * kk)), 1e-12)
    k2 = k * (1.0 + (a - 1.0) * ka_ref[...])
    bonus = segsum(r * k2 * rk_ref[...]) * v
    pre_ref[0] = r
    pre_ref[1] = k2
    pre_ref[2] = v
    pre_ref[3] = -kk
    pre_ref[4] = kk * a
    pre_ref[5] = lw

    c = CHUNK
    rowc = lax.broadcasted_iota(jnp.int32, (c, LANES), 0)
    colc = lax.broadcasted_iota(jnp.int32, (c, LANES), 1) % HEAD
    r2 = lax.broadcasted_iota(jnp.int32, (LANES, LANES), 0)
    c2 = lax.broadcasted_iota(jnp.int32, (LANES, LANES), 1)
    masks = (m0, colc < rowc, colc <= rowc, (colc == rowc).astype(F32),
             (r2 // HEAD) == (c2 // HEAD), r2 == c2)
    ltri = (lax.broadcasted_iota(jnp.int32, (c, c), 1) <= lax.broadcasted_iota(jnp.int32, (c, c), 0)).astype(BF16)
    n_pairs = width // LANES
    names = ("rt", "at", "kt", "bt", "bh", "kh", "v", "pc")
    ops = {nm: [] for nm in names}
    for ci in range(tl // c):
        rows = slice(ci * c, (ci + 1) * c)
        lwc = pre_ref[5, rows, :]
        cum = sum(_dot(ltri, part) for part in _split3(lwc))
        pw = jnp.exp(cum)
        pinv = jnp.exp(-cum)
        kt = pre_ref[1, rows, :] * pinv
        bt = pre_ref[4, rows, :] * pinv
        pc = pw[c - 1:c, :]
        full = {"rt": pre_ref[0, rows, :] * pw, "at": pre_ref[3, rows, :] * jnp.exp(cum - lwc), "kt": kt,
                "bt": bt, "bh": bt * pc, "kh": kt * pc, "v": pre_ref[2, rows, :], "pc": pc}
        for nm in names:
            ops[nm].append([full[nm][:, j * LANES:(j + 1) * LANES] for j in range(n_pairs)])
    ys, hs = _wkv_chunks(*[ops[nm] for nm in names], [h_ref[j] for j in range(n_pairs)], masks)
    for ci, yrow in enumerate(ys):
        for j, y in enumerate(yrow):
            yacc_ref[ci * c:(ci + 1) * c, j * LANES:(j + 1) * LANES] = y
    for j in range(n_pairs):
        h_ref[j] = hs[j]

    y = yacc_ref[...]
    mean = segsum(y) * (1.0 / HEAD)
    yc = y - mean
    var = segsum(yc * yc) * (1.0 / HEAD)
    yn = yc * lax.rsqrt(var + LNX_EPS) * lnw_ref[...] + lnb_ref[...]
    y_ref[0] = (yn + bonus) * g


def _rwkv(p3, mu, w0, wwa, a0, g2, k_k, k_a, r_k, lnx_w, lnx_b, ones_bd, width, tl):
    b, l, n_rwkv = p3.shape
    n_pairs = width // LANES
    consts = [mu, w0, wwa, a0, g2, k_k, k_a, r_k, lnx_w, lnx_b, ones_bd]
    return pl.pallas_call(
        functools.partial(_rwkv_body, tl, width),
        grid=(b, l // tl),
        in_specs=[pl.BlockSpec((1, tl, n_rwkv), lambda i, j: (i, j, 0))] + [_const_spec(c.shape) for c in consts],
        out_specs=pl.BlockSpec((1, tl, width), lambda i, j: (i, j, 0)),
        out_shape=jax.ShapeDtypeStruct((b, l, width), F32),
        scratch_shapes=[pltpu.VMEM((1, n_rwkv), F32), pltpu.VMEM((n_pairs, LANES, LANES), F32),
                        pltpu.VMEM((6, tl, width), F32), pltpu.VMEM((tl, width), F32)],
        compiler_params=_params(("arbitrary", "arbitrary")),
        name="rwkv7",
    )(p3, *consts)


def _s5_disc_body(are_ref, aim_ref, dt_ref, bre_ref, bim_ref, abr_ref, abi_ref, bbr_ref, bbi_ref):
    lre, lim, dt = are_ref[...], aim_ref[...], jnp.exp(dt_ref[...])
    mag = jnp.exp(lre * dt)
    abr = mag * jnp.cos(lim * dt)
    abi = mag * jnp.sin(lim * dt)
    inv = 1.0 / (lre * lre + lim * lim)
    nr, ni = abr - 1.0, abi
    cr = (nr * lre + ni * lim) * inv
    ci = (ni * lre - nr * lim) * inv
    bre, bim = bre_ref[...], bim_ref[...]
    abr_ref[...] = abr
    abi_ref[...] = abi
    bbr_ref[...] = cr * bre - ci * bim
    bbi_ref[...] = cr * bim + ci * bre


def _s5_disc(a_re, a_im, log_step, b_re, b_im):
    g, p, c = b_re.shape
    col = lambda x: x.reshape(g * p, 1)
    dt = jnp.broadcast_to(log_step[:, None], (g, p))
    outs = pl.pallas_call(
        _s5_disc_body,
        out_shape=[jax.ShapeDtypeStruct((g * p, 1), F32)] * 2 + [jax.ShapeDtypeStruct((g * p, c), F32)] * 2,
        name="s5_disc",
    )(col(a_re), col(a_im), col(dt), b_re.reshape(g * p, c), b_im.reshape(g * p, c))
    abr, abi, bbr, bbi = outs
    return abr.reshape(g, p), abi.reshape(g, p), bbr.reshape(g, p, c), bbi.reshape(g, p, c)


def _s5_body(tr, ns, col_w, u_ref, bblk_ref, are_ref, aim_ref, cblk_ref, d_ref, wglu_ref, bglu_ref,
             y_ref, st_ref, s_ref):
    @pl.when(pl.program_id(0) == 0)
    def _():
        s_ref[...] = jnp.zeros_like(s_ref)

    u = u_ref[...]
    st_ref[...] = jnp.dot(u.astype(BF16), bblk_ref[...], preferred_element_type=F32)
    nb = SUBLANES
    for cg in range(ns // col_w):
        re_c = slice(cg * col_w, (cg + 1) * col_w)
        im_c = slice(ns + cg * col_w, ns + (cg + 1) * col_w)
        are = jnp.broadcast_to(are_ref[:, re_c], (nb, col_w))
        aim = jnp.broadcast_to(aim_ref[:, re_c], (nb, col_w))

        def step(t, carry):
            sre, sim = carry
            rows = pl.ds(pl.multiple_of(t * nb, nb), nb)
            nre = are * sre - aim * sim + st_ref[rows, re_c]
            nim = are * sim + aim * sre + st_ref[rows, im_c]
            st_ref[rows, re_c] = nre
            st_ref[rows, im_c] = nim
            return nre, nim

        sre, sim = lax.fori_loop(0, tr // nb, step, (s_ref[:, re_c], s_ref[:, im_c]), unroll=4)
        s_ref[:, re_c] = sre
        s_ref[:, im_c] = sim
    y = jnp.dot(st_ref[...].astype(BF16), cblk_ref[...], preferred_element_type=F32) + d_ref[...] * u
    y = jax.nn.gelu(y)
    y_ref[...] = y * jax.nn.sigmoid(jnp.dot(y.astype(BF16), wglu_ref[...], preferred_element_type=F32)
                                    + bglu_ref[...])


def _s5(u_tm, bblk, are, aim, cblk, d, w_glu, b_glu, tr):
    t, width = u_tm.shape
    ns = are.shape[1]
    consts = [bblk, are, aim, cblk, d, w_glu, b_glu]
    row = lambda i: (i, 0)
    return pl.pallas_call(
        functools.partial(_s5_body, tr, ns, 4 * LANES),
        grid=(t // tr,),
        in_specs=[pl.BlockSpec((tr, width), row)] + [_const_spec(c.shape) for c in consts],
        out_specs=pl.BlockSpec((tr, width), row),
        out_shape=jax.ShapeDtypeStruct((t, width), F32),
        scratch_shapes=[pltpu.VMEM((tr, 2 * ns), F32), pltpu.VMEM((SUBLANES, 2 * ns), F32)],
        compiler_params=_params(("arbitrary",)),
        name="s5",
    )(u_tm, *consts)


def _merge_body(d, x_ref, yr_ref, ys_ref, gate_ref, wbr_ref, wbs_ref, wout_ref, g_ref, o_ref):
    o_r = jnp.dot(yr_ref[...].astype(BF16), wbr_ref[...], preferred_element_type=F32)
    o_s = jnp.dot(ys_ref[...].astype(BF16), wbs_ref[...], preferred_element_type=F32)
    mixed = gate_ref[:, :d] * o_r + gate_ref[:, d:] * o_s
    mixed = jnp.dot(mixed.astype(BF16), wout_ref[...], preferred_element_type=F32)
    o_ref[...] = x_ref[...] + _rms(mixed, g_ref[...])


def _merge(x2, yr, ys, gates, wbr, wbs, wout, g, tm):
    t, d = x2.shape
    row = lambda i: (i, 0)
    consts = [wbr, wbs, wout, g]
    return pl.pallas_call(
        functools.partial(_merge_body, d),
        grid=(t // tm,),
        in_specs=[pl.BlockSpec((tm, d), row), pl.BlockSpec((tm, yr.shape[1]), row),
                  pl.BlockSpec((tm, ys.shape[1]), row), pl.BlockSpec((tm, 2 * d), row)]
                 + [_const_spec(c.shape) for c in consts],
        out_specs=pl.BlockSpec((tm, d), row),
        out_shape=jax.ShapeDtypeStruct((t, d), F32),
        compiler_params=_params(("arbitrary",)),
        name="merge",
    )(x2, yr, ys, gates, *consts)


def _ffn_body(tm, f, cb, x_ref, gpre_ref, wup_ref, cw_ref, cbias_ref, wdown_ref, gpost_ref, o_ref,
              carry_ref, act_ref):
    @pl.when(pl.program_id(1) == 0)
    def _():
        carry_ref[...] = jnp.zeros_like(carry_ref)

    x = x_ref[0]
    h = _rms(x, gpre_ref[...]).astype(BF16)
    nb = SUBLANES
    sub = lax.broadcasted_iota(jnp.int32, (nb, 1), 0)

    def conv(cols):
        z = jnp.dot(h, wup_ref[:, cols], preferred_element_type=F32)
        tail = carry_ref[:, cols]
        carry_ref[:, cols] = z[tm - nb:, :]
        out = cbias_ref[:, cols] + cw_ref[CONV_WIDTH - 1:CONV_WIDTH, cols] * z
        for s in range(1, CONV_WIDTH):
            zr = pltpu.roll(z, s, 0)
            top = jnp.where(sub < s, pltpu.roll(tail, s, 0), zr[:nb])
            zs = jnp.concatenate([top, zr[nb:]], axis=0)
            out = out + cw_ref[CONV_WIDTH - 1 - s:CONV_WIDTH - s, cols] * zs
        return out

    for j in range(f // cb):
        gate = conv(slice(j * cb, (j + 1) * cb))
        val = conv(slice(f + j * cb, f + (j + 1) * cb))
        act_ref[:, j * cb:(j + 1) * cb] = (jax.nn.gelu(gate) * val).astype(BF16)
    y = jnp.dot(act_ref[...], wdown_ref[...], preferred_element_type=F32)
    o_ref[0] = x + _rms(y, gpost_ref[...])


def _ffn(x3, g_pre, w_up, conv_w, conv_b, w_down, g_post, tm):
    b, l, d = x3.shape
    f = w_down.shape[0]
    cb = 2 * LANES
    consts = [g_pre, w_up, conv_w, conv_b, w_down, g_post]
    return pl.pallas_call(
        functools.partial(_ffn_body, tm, f, cb),
        grid=(b, l // tm),
        in_specs=[pl.BlockSpec((1, tm, d), lambda i, j: (i, j, 0))] + [_const_spec(c.shape) for c in consts],
        out_specs=pl.BlockSpec((1, tm, d), lambda i, j: (i, j, 0)),
        out_shape=jax.ShapeDtypeStruct((b, l, d), F32),
        scratch_shapes=[pltpu.VMEM((SUBLANES, 2 * f), F32), pltpu.VMEM((tm, f), BF16)],
        compiler_params=_params(("arbitrary", "arbitrary")),
        name="conv_ffn",
    )(x3, *consts)


def _block_diag(blocks):
    g, r, c = blocks.shape
    eye = jnp.eye(g, dtype=blocks.dtype)
    return (blocks[:, :, None, :] * eye[:, None, :, None]).reshape(g * r, g * c)


def _layer(x, norm_mix_pre, norm_mix_post, norm_ffn_pre, norm_ffn_post, w_in, b_gate, mu, w0, w2, a0, a2, g2,
           k_k, k_a, r_k, lnx_w, lnx_b, s5_a_re, s5_a_im, s5_b_re, s5_b_im, s5_c_re, s5_c_im, s5_d, s5_log_step,
           s5_w_glu, s5_b_glu, w_branch_rwkv, w_branch_s5, w_out, ffn_w_up, ffn_conv_w, ffn_conv_b, ffn_w_down,
           tiles):
    b, l, d = x.shape
    t = b * l
    width = w0.shape[0]
    s5_width = s5_d.shape[0]
    n_rwkv = mu.shape[0]
    rank_w, rank_a = w2.shape[0], a2.shape[0]
    assert rank_w == HEAD and rank_a == HEAD and g2.shape[0] == LANES and b == SUBLANES
    rowv = lambda vec: vec.reshape(1, -1)
    x2 = x.reshape(t, d)

    p, u, gates = _in_proj(x2, rowv(norm_mix_pre), w_in.astype(BF16), rowv(b_gate), n_rwkv, s5_width,
                           tiles["in"])

    wwa = jnp.zeros((LANES, 2 * width), F32).at[:rank_w, :width].set(w2).at[rank_w:, width:].set(a2)
    ones_bd = _block_diag(jnp.ones((width // HEAD, HEAD, HEAD), F32)).astype(BF16)
    y_rwkv = _rwkv(p.reshape(b, l, n_rwkv), rowv(mu), rowv(w0), wwa.astype(BF16), rowv(a0), g2.astype(BF16),
                   rowv(k_k), rowv(k_a), rowv(r_k), rowv(lnx_w), rowv(lnx_b), ones_bd, width,
                   tiles["rwkv"]).reshape(t, width)

    abr, abi, bbr, bbi = _s5_disc(s5_a_re, s5_a_im, s5_log_step, s5_b_re, s5_b_im)
    bblk = jnp.concatenate([_block_diag(bbr.transpose(0, 2, 1)), _block_diag(bbi.transpose(0, 2, 1))], axis=1)
    cblk = jnp.concatenate([_block_diag(s5_c_re.transpose(0, 2, 1)), -_block_diag(s5_c_im.transpose(0, 2, 1))],
                           axis=0)
    u_tm = u.reshape(b, l, s5_width).transpose(1, 0, 2).reshape(t, s5_width)
    y_s5 = _s5(u_tm, bblk.astype(BF16), abr.reshape(1, -1), abi.reshape(1, -1), cblk.astype(BF16), rowv(s5_d),
               s5_w_glu.astype(BF16), rowv(s5_b_glu), tiles["s5"])
    y_s5 = y_s5.reshape(l, b, s5_width).transpose(1, 0, 2).reshape(t, s5_width)

    x1 = _merge(x2, y_rwkv, y_s5, gates, w_branch_rwkv.astype(BF16), w_branch_s5.astype(BF16),
                w_out.astype(BF16), rowv(norm_mix_post), tiles["merge"])
    out = _ffn(x1.reshape(b, l, d), rowv(norm_ffn_pre), ffn_w_up.astype(BF16), ffn_conv_w, rowv(ffn_conv_b),
               ffn_w_down.astype(BF16), rowv(norm_ffn_post), tiles["ffn"])
    return out


TILES = {"in": 256, "rwkv": 256, "s5": 256, "merge": 256, "ffn": 256}


def kernel(x, norm_mix_pre, norm_mix_post, norm_ffn_pre, norm_ffn_post, w_in, b_gate, rwkv_shift_mu, rwkv_w0, rwkv_w2, rwkv_a0, rwkv_a2, rwkv_g2, rwkv_k_k, rwkv_k_a, rwkv_r_k, rwkv_lnx_w, rwkv_lnx_b, s5_a_re, s5_a_im, s5_b_re, s5_b_im, s5_c_re, s5_c_im, s5_d, s5_log_step, s5_w_glu, s5_b_glu, w_branch_rwkv, w_branch_s5, w_out, ffn_w_up, ffn_conv_w, ffn_conv_b, ffn_w_down):
    depth = w_in.shape[0]
    for i in range(depth):
        x = _layer(x, norm_mix_pre[i], norm_mix_post[i], norm_ffn_pre[i], norm_ffn_post[i], w_in[i], b_gate[i],
                   rwkv_shift_mu[i], rwkv_w0[i], rwkv_w2[i], rwkv_a0[i], rwkv_a2[i], rwkv_g2[i], rwkv_k_k[i],
                   rwkv_k_a[i], rwkv_r_k[i].reshape(-1), rwkv_lnx_w[i], rwkv_lnx_b[i], s5_a_re[i], s5_a_im[i],
                   s5_b_re[i], s5_b_im[i], s5_c_re[i], s5_c_im[i], s5_d[i], s5_log_step[i], s5_w_glu[i],
                   s5_b_glu[i], w_branch_rwkv[i], w_branch_s5[i], w_out[i], ffn_w_up[i], ffn_conv_w[i],
                   ffn_conv_b[i], ffn_w_down[i], TILES)
    return x
```

```python
import functools
import math

import jax
import jax.numpy as jnp
from jax import lax
from jax.experimental import pallas as pl
from jax.experimental.pallas import tpu as pltpu

F32 = jnp.float32
BF16 = jnp.bfloat16

NORM_EPS = 1e-6
LNX_EPS = 64e-5
HEAD = 64
CHUNK = 64
LANES = 128
SUBLANES = 8
S5_GROUP = 16
S5_STATE = 64
CONV_WIDTH = 3
MXU_K = 256
VMEM_LIMIT = 56 * 1024 * 1024


def _rms(x, g):
    return x * lax.rsqrt(jnp.mean(x * x, axis=-1, keepdims=True) + NORM_EPS) * g


def _split(x):
    hi = x.astype(BF16)
    lo = (x - hi.astype(F32)).astype(BF16)
    return hi, lo


_NN = (((1,), (0,)), ((), ()))
_NT = (((1,), (1,)), ((), ()))
_TN = (((0,), (0,)), ((), ()))


def _mm(a, b, dims=_NN, passes=1):
    d = lambda p, q: lax.dot_general(p, q, dims, preferred_element_type=F32)
    if passes == 1:
        return d(a.astype(BF16), b.astype(BF16))
    ah, al = _split(a)
    if passes == 2:
        bh = b.astype(BF16)
        return d(ah, bh) + d(al, bh)
    bh, bl = _split(b)
    return d(ah, bh) + (d(ah, bl) + d(al, bh))


def _const_spec(shape):
    nd = len(shape)
    return pl.BlockSpec(shape, lambda *_: (0,) * nd)


def _params(sem):
    return pltpu.CompilerParams(dimension_semantics=sem, vmem_limit_bytes=VMEM_LIMIT)


def _in_proj_body(n_rwkv, n_s5, x_ref, g_ref, w_ref, bg_ref, p_ref, u_ref, gate_ref):
    h = _rms(x_ref[...], g_ref[...]).astype(BF16)
    p_ref[...] = jnp.dot(h, w_ref[:, :n_rwkv], preferred_element_type=F32)
    u_ref[...] = jnp.dot(h, w_ref[:, n_rwkv:n_rwkv + n_s5], preferred_element_type=F32)
    gate_ref[...] = jax.nn.sigmoid(
        jnp.dot(h, w_ref[:, n_rwkv + n_s5:], preferred_element_type=F32) + bg_ref[...]).astype(BF16)


def _in_proj(x2, g, w_in, b_gate, n_rwkv, n_s5, tm):
    t, d = x2.shape
    n_in = w_in.shape[1]
    n_gate = n_in - n_rwkv - n_s5
    row = lambda i: (i, 0)
    return pl.pallas_call(
        functools.partial(_in_proj_body, n_rwkv, n_s5),
        grid=(t // tm,),
        in_specs=[pl.BlockSpec((tm, d), row), _const_spec((1, d)), _const_spec((d, n_in)),
                  _const_spec((1, n_gate))],
        out_specs=[pl.BlockSpec((tm, n_rwkv), row), pl.BlockSpec((tm, n_s5), row),
                   pl.BlockSpec((tm, n_gate), row)],
        out_shape=[jax.ShapeDtypeStruct((t, n_rwkv), F32), jax.ShapeDtypeStruct((t, n_s5), F32),
                   jax.ShapeDtypeStruct((t, n_gate), BF16)],
        compiler_params=_params(("arbitrary",)),
        name="in_proj",
    )(x2, g, w_in, b_gate)


def _mstack(x, m0):
    z = jnp.zeros_like(x)
    return jnp.concatenate([jnp.where(m0, x, z), jnp.where(m0, z, x)], axis=0)


def _dot(a, b, dims=_NN):
    return lax.dot_general(a, b, dims, preferred_element_type=F32)


def _cat0(*xs):
    return jnp.concatenate(xs, axis=0)


def _cat1(*xs):
    return jnp.concatenate(xs, axis=1)


def _wkv_chunks(rt, at, kt, bt, bh, kh, v, pc, hs, masks):
    m0, strict, incl, eye, bd, diag = masks
    c = CHUNK
    ms = lambda x: _mstack(x, m0)
    b16 = lambda x: x.astype(BF16)
    ids = [(ci, j) for ci in range(len(rt)) for j in range(len(rt[0]))]
    pick = lambda nested: [nested[ci][j] for ci, j in ids]
    rt32 = pick(rt)
    rt, at, kt, bt, bh, kh, v = [[b16(x) for x in pick(n)] for n in (rt, at, kt, bt, bh, kh, v)]
    n = range(len(ids))
    msv = [ms(x) for x in v]
    g = [_dot(_cat0(at[i], rt[i]), _cat0(ms(bt[i]), ms(kt[i])), _NT) for i in n]
    a_ab = [jnp.where(strict, x[:c, :LANES], 0.0) for x in g]
    a_ak = [b16(jnp.where(strict, x[:c, LANES:], 0.0)) for x in g]
    a_rbk = [b16(_cat1(jnp.where(incl, x[c:, :LANES], 0.0), jnp.where(incl, x[c:, LANES:], 0.0))) for x in g]
    ab16 = [b16(x) for x in a_ab]
    q = [b16(_dot(ab16[i], ms(ab16[i]))) for i in n]
    t_inv = [eye + x for x in a_ab]
    n_sq = int(math.log2(c)) - 1
    for s in range(n_sq - 1):
        tq = [_dot(_cat0(b16(t_inv[i]), q[i]), ms(q[i])) for i in n]
        t_inv = [t_inv[i] + tq[i][:c] for i in n]
        q = [b16(tq[i][c:]) for i in n]
    t16 = [b16(t_inv[i]) for i in n]
    t_inv = [b16(t_inv[i] + _dot(t16[i], ms(q[i]))) for i in n]
    w0 = [b16(_dot(a_ak[i], msv[i])) for i in n]
    x1 = [_dot(t_inv[i], _cat1(ms(at[i]), ms(w0[i]))) for i in n]
    ap = [b16(x[:, :LANES]) for x in x1]
    u0 = [b16(x[:, LANES:]) for x in x1]
    zero = jnp.zeros((2 * c, LANES), BF16)
    ry = [_dot(a_rbk[i], _cat0(_cat1(ms(ap[i]), ms(u0[i])), _cat1(zero, msv[i]))) for i in n]
    mh = [_dot(_cat0(bh[i], kh[i]), _cat0(_cat1(ap[i], u0[i]), _cat1(zero[:c], v[i])), _TN) for i in n]
    rp = [b16(rt32[i] + ry[i][:, :LANES]) for i in n]
    ys, hs = [], list(hs)
    for i, (ci, j) in enumerate(ids):
        m = jnp.where(bd, mh[i][:, :LANES], 0.0) + jnp.where(diag, pc[ci][j], 0.0)
        yh = _dot(_cat0(rp[i], b16(m)), b16(hs[j]))
        ys.append(yh[:c] + ry[i][:, LANES:])
        hs[j] = yh[c:] + jnp.where(bd, mh[i][:, LANES:], 0.0)
    n_pairs = len(hs)
    return [ys[ci * n_pairs:(ci + 1) * n_pairs] for ci in range(len(ids) // n_pairs)], hs


def _split3(x):
    hi = x.astype(BF16)
    r1 = x - hi.astype(F32)
    mid = r1.astype(BF16)
    return hi, mid, (r1 - mid.astype(F32)).astype(BF16)


def _rwkv_body(tl, width, p_ref, mu_ref, w0_ref, wwa_ref, a0_ref, g2_ref, kk_ref, ka_ref, rk_ref,
               lnw_ref, lnb_ref, ones_ref, y_ref, carry_ref, h_ref, pre_ref, yacc_ref):
    li = pl.program_id(1)

    @pl.when(li == 0)
    def _():
        carry_ref[...] = jnp.zeros_like(carry_ref)
        h_ref[...] = jnp.zeros_like(h_ref)

    w3 = 3 * width
    p = p_ref[0]
    row = lax.broadcasted_iota(jnp.int32, (tl, 1), 0)
    prev = jnp.where(row == 0, carry_ref[...], pltpu.roll(p, 1, 0))
    carry_ref[...] = p[tl - 1:tl, :]
    ps = p + (prev - p) * mu_ref[...]
    r = ps[:, :width]
    k = ps[:, width:2 * width]
    v = ps[:, 2 * width:w3]
    wa = ps[:, w3:w3 + LANES]
    gd = ps[:, w3 + LANES:]
    lane = lax.broadcasted_iota(jnp.int32, (1, LANES), 1)
    m0 = lane < HEAD
    twa = _mm(jnp.where(m0, jnp.tanh(wa), wa), wwa_ref[...])
    w = -jax.nn.softplus(-(w0_ref[...] + twa[:, :width])) - 0.5
    lw = -jnp.exp(w)
    a = jax.nn.sigmoid(a0_ref[...] + twa[:, width:])
    g = _mm(jax.nn.sigmoid(gd), g2_ref[...])
    ones = ones_ref[...]
    segsum = lambda x: _mm(x, ones)
    kk = k * kk_ref[...]
    kk = kk / jnp.maximum(jnp.sqrt(segsum(kk * kk)), 1e-12)
    k2 = k * (1.0 + (a - 1.0) * ka_ref[...])
    bonus = segsum(r * k2 * rk_ref[...]) * v
    pre_ref[0] = r
    pre_ref[1] = k2
    pre_ref[2] = v
    pre_ref[3] = -kk
    pre_ref[4] = kk * a
    pre_ref[5] = lw

    c = CHUNK
    rowc = lax.broadcasted_iota(jnp.int32, (c, LANES), 0)
    colc = lax.broadcasted_iota(jnp.int32, (c, LANES), 1) % HEAD
    r2 = lax.broadcasted_iota(jnp.int32, (LANES, LANES), 0)
    c2 = lax.broadcasted_iota(jnp.int32, (LANES, LANES), 1)
    masks = (m0, colc < rowc, colc <= rowc, (colc == rowc).astype(F32),
             (r2 // HEAD) == (c2 // HEAD), r2 == c2)
    ltri = (lax.broadcasted_iota(jnp.int32, (c, c), 1) <= lax.broadcasted_iota(jnp.int32, (c, c), 0)).astype(BF16)
    ltri3 = _cat1(ltri, ltri, ltri)
    n_pairs = width // LANES
    names = ("rt", "at", "kt", "bt", "bh", "kh", "v", "pc")
    ops = {nm: [] for nm in names}
    for ci in range(tl // c):
        rows = slice(ci * c, (ci + 1) * c)
        lwc = pre_ref[5, rows, :]
        cum = _dot(ltri3, _cat0(*_split3(lwc)))
        pw = jnp.exp(cum)
        pinv = jnp.exp(-cum)
        kt = pre_ref[1, rows, :] * pinv
        bt = pre_ref[4, rows, :] * pinv
        pc = pw[c - 1:c, :]
        full = {"rt": pre_ref[0, rows, :] * pw, "at": pre_ref[3, rows, :] * jnp.exp(cum - lwc), "kt": kt,
                "bt": bt, "bh": bt * pc, "kh": kt * pc, "v": pre_ref[2, rows, :], "pc": pc}
        for nm in names:
            ops[nm].append([full[nm][:, j * LANES:(j + 1) * LANES] for j in range(n_pairs)])
    ys, hs = _wkv_chunks(*[ops[nm] for nm in names], [h_ref[j] for j in range(n_pairs)], masks)
    for ci, yrow in enumerate(ys):
        for j, y in enumerate(yrow):
            yacc_ref[ci * c:(ci + 1) * c, j * LANES:(j + 1) * LANES] = y
    for j in range(n_pairs):
        h_ref[j] = hs[j]

    y = yacc_ref[...]
    mean = segsum(y) * (1.0 / HEAD)
    yc = y - mean
    var = segsum(yc * yc) * (1.0 / HEAD)
    yn = yc * lax.rsqrt(var + LNX_EPS) * lnw_ref[...] + lnb_ref[...]
    y_ref[0] = ((yn + bonus) * g).astype(BF16)


def _rwkv(p3, mu, w0, wwa, a0, g2, k_k, k_a, r_k, lnx_w, lnx_b, ones_bd, width, tl):
    b, l, n_rwkv = p3.shape
    n_pairs = width // LANES
    consts = [mu, w0, wwa, a0, g2, k_k, k_a, r_k, lnx_w, lnx_b, ones_bd]
    return pl.pallas_call(
        functools.partial(_rwkv_body, tl, width),
        grid=(b, l // tl),
        in_specs=[pl.BlockSpec((1, tl, n_rwkv), lambda i, j: (i, j, 0))] + [_const_spec(c.shape) for c in consts],
        out_specs=pl.BlockSpec((1, tl, width), lambda i, j: (i, j, 0)),
        out_shape=jax.ShapeDtypeStruct((b, l, width), BF16),
        scratch_shapes=[pltpu.VMEM((1, n_rwkv), F32), pltpu.VMEM((n_pairs, LANES, LANES), F32),
                        pltpu.VMEM((6, tl, width), F32), pltpu.VMEM((tl, width), F32)],
        compiler_params=_params(("arbitrary", "arbitrary")),
        name="rwkv7",
    )(p3, *consts)


def _s5_disc_body(are_ref, aim_ref, dt_ref, bre_ref, bim_ref, abr_ref, abi_ref, bbr_ref, bbi_ref):
    lre, lim, dt = are_ref[...], aim_ref[...], jnp.exp(dt_ref[...])
    mag = jnp.exp(lre * dt)
    abr = mag * jnp.cos(lim * dt)
    abi = mag * jnp.sin(lim * dt)
    inv = 1.0 / (lre * lre + lim * lim)
    nr, ni = abr - 1.0, abi
    cr = (nr * lre + ni * lim) * inv
    ci = (ni * lre - nr * lim) * inv
    bre, bim = bre_ref[...], bim_ref[...]
    abr_ref[...] = abr
    abi_ref[...] = abi
    bbr_ref[...] = cr * bre - ci * bim
    bbi_ref[...] = cr * bim + ci * bre


def _s5_disc(a_re, a_im, log_step, b_re, b_im):
    g, p, c = b_re.shape
    col = lambda x: x.reshape(g * p, 1)
    dt = jnp.broadcast_to(log_step[:, None], (g, p))
    outs = pl.pallas_call(
        _s5_disc_body,
        out_shape=[jax.ShapeDtypeStruct((g * p, 1), F32)] * 2 + [jax.ShapeDtypeStruct((g * p, c), F32)] * 2,
        name="s5_disc",
    )(col(a_re), col(a_im), col(dt), b_re.reshape(g * p, c), b_im.reshape(g * p, c))
    abr, abi, bbr, bbi = outs
    return abr.reshape(g, p), abi.reshape(g, p), bbr.reshape(g, p, c), bbi.reshape(g, p, c)


def _s5_body(tt, nh, col_w, u_ref, perm_ref, permt_ref, b_ref, are_ref, aim_ref, c_ref, d_ref, wglu_ref, bglu_ref,
             y_ref, st_ref, s_ref):
    @pl.when(pl.program_id(0) == 0)
    def _():
        s_ref[...] = jnp.zeros_like(s_ref)

    nb = SUBLANES
    rows = nb * tt
    width = u_ref.shape[-1]
    kin = width // nh
    nsh = are_ref.shape[-1]
    u = u_ref[...].reshape(rows, width)
    u_tm = jnp.dot(perm_ref[...], u.astype(BF16), preferred_element_type=F32).astype(BF16)
    for h in range(nh):
        st_ref[h] = jnp.dot(u_tm[:, h * kin:(h + 1) * kin], b_ref[h], preferred_element_type=F32)
    for h in range(nh):
        for cg in range(nsh // col_w):
            re_c = slice(cg * col_w, (cg + 1) * col_w)
            im_c = slice(nsh + cg * col_w, nsh + (cg + 1) * col_w)
            are = jnp.broadcast_to(are_ref[h, :, re_c], (nb, col_w))
            aim = jnp.broadcast_to(aim_ref[h, :, re_c], (nb, col_w))

            def step(t, carry):
                sre, sim = carry
                r8 = pl.ds(pl.multiple_of(t * nb, nb), nb)
                nre = are * sre - aim * sim + st_ref[h, r8, re_c]
                nim = are * sim + aim * sre + st_ref[h, r8, im_c]
                st_ref[h, r8, re_c] = nre
                st_ref[h, r8, im_c] = nim
                return nre, nim

            sre, sim = lax.fori_loop(0, tt, step, (s_ref[h, :, re_c], s_ref[h, :, im_c]), unroll=4)
            s_ref[h, :, re_c] = sre
            s_ref[h, :, im_c] = sim
    y_tm = jnp.concatenate([jnp.dot(st_ref[h].astype(BF16), c_ref[h], preferred_element_type=F32)
                            for h in range(nh)], axis=1)
    permt = permt_ref[...]
    y = _dot(_cat1(permt, permt), _cat0(*_split(y_tm)))
    y = jax.nn.gelu(y + d_ref[...] * u)
    y = y * jax.nn.sigmoid(jnp.dot(y.astype(BF16), wglu_ref[...], preferred_element_type=F32) + bglu_ref[...])
    y_ref[...] = y.astype(BF16).reshape(nb, tt, width)


def _s5(u3, perm, b_h, are, aim, c_h, d, w_glu, b_glu, tt):
    b, l, width = u3.shape
    nh, _, nsh2 = b_h.shape
    consts = [perm, perm.T, b_h, are, aim, c_h, d, w_glu, b_glu]
    blk = pl.BlockSpec((b, tt, width), lambda i: (0, i, 0))
    return pl.pallas_call(
        functools.partial(_s5_body, tt, nh, 4 * LANES),
        grid=(l // tt,),
        in_specs=[blk] + [_const_spec(c.shape) for c in consts],
        out_specs=blk,
        out_shape=jax.ShapeDtypeStruct((b, l, width), BF16),
        scratch_shapes=[pltpu.VMEM((nh, b * tt, nsh2), F32), pltpu.VMEM((nh, b, nsh2), F32)],
        compiler_params=_params(("arbitrary",)),
        name="s5",
    )(u3, *consts)


def _mix_ffn_body(tm, d, f, cb, x_ref, yr_ref, ys_ref, gate_ref, wbr_ref, wbs_ref, wout_ref, gmix_ref,
                  gpre_ref, wup_ref, cw_ref, cbias_ref, wdown_ref, gpost_ref, o_ref, carry_ref, act_ref):
    @pl.when(pl.program_id(1) == 0)
    def _():
        carry_ref[...] = jnp.zeros_like(carry_ref)

    o_r = jnp.dot(yr_ref[0], wbr_ref[...], preferred_element_type=F32)
    o_s = jnp.dot(ys_ref[0], wbs_ref[...], preferred_element_type=F32)
    mixed = gate_ref[0, :, :d] * o_r + gate_ref[0, :, d:] * o_s
    mixed = jnp.dot(mixed.astype(BF16), wout_ref[...], preferred_element_type=F32)
    x = x_ref[0] + _rms(mixed, gmix_ref[...])

    h = _rms(x, gpre_ref[...]).astype(BF16)
    nb = SUBLANES
    sub = lax.broadcasted_iota(jnp.int32, (nb, 1), 0)

    def conv(cols):
        z = jnp.dot(h, wup_ref[:, cols], preferred_element_type=F32)
        tail = carry_ref[:, cols]
        carry_ref[:, cols] = z[tm - nb:, :]
        out = cbias_ref[:, cols] + cw_ref[CONV_WIDTH - 1:CONV_WIDTH, cols] * z
        for s in range(1, CONV_WIDTH):
            zr = pltpu.roll(z, s, 0)
            top = jnp.where(sub < s, pltpu.roll(tail, s, 0), zr[:nb])
            zs = jnp.concatenate([top, zr[nb:]], axis=0)
            out = out + cw_ref[CONV_WIDTH - 1 - s:CONV_WIDTH - s, cols] * zs
        return out

    for j in range(f // cb):
        gate = conv(slice(j * cb, (j + 1) * cb))
        val = conv(slice(f + j * cb, f + (j + 1) * cb))
        act_ref[:, j * cb:(j + 1) * cb] = (jax.nn.gelu(gate) * val).astype(BF16)
    y = jnp.dot(act_ref[...], wdown_ref[...], preferred_element_type=F32)
    o_ref[0] = x + _rms(y, gpost_ref[...])


def _mix_ffn(x3, yr, ys, gates, wbr, wbs, wout, g_mix, g_pre, w_up, conv_w, conv_b, w_down, g_post, tm):
    b, l, d = x3.shape
    f = w_down.shape[0]
    cb = 2 * LANES
    consts = [wbr, wbs, wout, g_mix, g_pre, w_up, conv_w, conv_b, w_down, g_post]
    tile = lambda a: pl.BlockSpec((1, tm, a.shape[-1]), lambda i, j: (i, j, 0))
    acts = [x3, yr, ys, gates]
    return pl.pallas_call(
        functools.partial(_mix_ffn_body, tm, d, f, cb),
        grid=(b, l // tm),
        in_specs=[tile(a) for a in acts] + [_const_spec(c.shape) for c in consts],
        out_specs=tile(x3),
        out_shape=jax.ShapeDtypeStruct((b, l, d), F32),
        scratch_shapes=[pltpu.VMEM((SUBLANES, 2 * f), F32), pltpu.VMEM((tm, f), BF16)],
        compiler_params=_params(("arbitrary", "arbitrary")),
        name="mix_ffn",
    )(*acts, *consts)


def _block_diag(blocks):
    g, r, c = blocks.shape
    eye = jnp.eye(g, dtype=blocks.dtype)
    return (blocks[:, :, None, :] * eye[:, None, :, None]).reshape(g * r, g * c)


def _layer(x, norm_mix_pre, norm_mix_post, norm_ffn_pre, norm_ffn_post, w_in, b_gate, mu, w0, w2, a0, a2, g2,
           k_k, k_a, r_k, lnx_w, lnx_b, s5_a_re, s5_a_im, s5_b_re, s5_b_im, s5_c_re, s5_c_im, s5_d, s5_log_step,
           s5_w_glu, s5_b_glu, w_branch_rwkv, w_branch_s5, w_out, ffn_w_up, ffn_conv_w, ffn_conv_b, ffn_w_down,
           tiles):
    b, l, d = x.shape
    t = b * l
    width = w0.shape[0]
    s5_width = s5_d.shape[0]
    n_rwkv = mu.shape[0]
    rank_w, rank_a = w2.shape[0], a2.shape[0]
    assert rank_w == HEAD and rank_a == HEAD and g2.shape[0] == LANES and b == SUBLANES
    rowv = lambda vec: vec.reshape(1, -1)
    x2 = x.reshape(t, d)

    p, u, gates = _in_proj(x2, rowv(norm_mix_pre), w_in.astype(BF16), rowv(b_gate), n_rwkv, s5_width,
                           tiles["in"])

    wwa = jnp.zeros((LANES, 2 * width), F32).at[:rank_w, :width].set(w2).at[rank_w:, width:].set(a2)
    ones_bd = _block_diag(jnp.ones((width // HEAD, HEAD, HEAD), F32)).astype(BF16)
    y_rwkv = _rwkv(p.reshape(b, l, n_rwkv), rowv(mu), rowv(w0), wwa.astype(BF16), rowv(a0), g2.astype(BF16),
                   rowv(k_k), rowv(k_a), rowv(r_k), rowv(lnx_w), rowv(lnx_b), ones_bd, width,
                   tiles["rwkv"]).reshape(t, width)

    abr, abi, bbr, bbi = _s5_disc(s5_a_re, s5_a_im, s5_log_step, s5_b_re, s5_b_im)
    n_grp = s5_a_re.shape[0]
    gh = MXU_K // S5_GROUP
    nh = n_grp // gh
    halves = lambda x: x.reshape((nh, gh) + x.shape[1:])
    bd = jax.vmap(_block_diag)
    tr = lambda x: halves(x).transpose(0, 1, 3, 2)
    b_h = jnp.concatenate([bd(tr(bbr)), bd(tr(bbi))], axis=2)
    c_h = jnp.concatenate([bd(tr(s5_c_re)), -bd(tr(s5_c_im))], axis=1)
    tt = tiles["s5"]
    ridx = jnp.arange(b * tt)
    perm = (ridx[None, :] == ((ridx % b) * tt + ridx // b)[:, None]).astype(BF16)
    y_s5 = _s5(u.reshape(b, l, s5_width), perm, b_h.astype(BF16), abr.reshape(nh, 1, -1), abi.reshape(nh, 1, -1),
               c_h.astype(BF16), rowv(s5_d), s5_w_glu.astype(BF16), rowv(s5_b_glu), tt).reshape(t, s5_width)

    per_b = lambda a: a.reshape(b, l, a.shape[-1])
    return _mix_ffn(x, per_b(y_rwkv), per_b(y_s5), per_b(gates), w_branch_rwkv.astype(BF16),
                    w_branch_s5.astype(BF16), w_out.astype(BF16), rowv(norm_mix_post), rowv(norm_ffn_pre),
                    ffn_w_up.astype(BF16), ffn_conv_w, rowv(ffn_conv_b), ffn_w_down.astype(BF16),
                    rowv(norm_ffn_post), tiles["ffn"])


TILES = {"in": 512, "rwkv": 256, "s5": 32, "ffn": 512}


def kernel(x, norm_mix_pre, norm_mix_post, norm_ffn_pre, norm_ffn_post, w_in, b_gate, rwkv_shift_mu, rwkv_w0, rwkv_w2, rwkv_a0, rwkv_a2, rwkv_g2, rwkv_k_k, rwkv_k_a, rwkv_r_k, rwkv_lnx_w, rwkv_lnx_b, s5_a_re, s5_a_im, s5_b_re, s5_b_im, s5_c_re, s5_c_im, s5_d, s5_log_step, s5_w_glu, s5_b_glu, w_branch_rwkv, w_branch_s5, w_out, ffn_w_up, ffn_conv_w, ffn_conv_b, ffn_w_down):
    depth = w_in.shape[0]
    for i in range(depth):
        x = _layer(x, norm_mix_pre[i], norm_mix_post[i], norm_ffn_pre[i], norm_ffn_post[i], w_in[i], b_gate[i],
                   rwkv_shift_mu[i], rwkv_w0[i], rwkv_w2[i], rwkv_a0[i], rwkv_a2[i], rwkv_g2[i], rwkv_k_k[i],
                   rwkv_k_a[i], rwkv_r_k[i].reshape(-1), rwkv_lnx_w[i], rwkv_lnx_b[i], s5_a_re[i], s5_a_im[i],
                   s5_b_re[i], s5_b_im[i], s5_c_re[i], s5_c_im[i], s5_d[i], s5_log_step[i], s5_w_glu[i],
                   s5_b_glu[i], w_branch_rwkv[i], w_branch_s5[i], w_out[i], ffn_w_up[i], ffn_conv_w[i],
                   ffn_conv_b[i], ffn_w_down[i], TILES)
    return x
```

```python
import functools
import math

import jax
import jax.numpy as jnp
from jax import lax
from jax.experimental import pallas as pl
from jax.experimental.pallas import tpu as pltpu

F32 = jnp.float32
BF16 = jnp.bfloat16

NORM_EPS = 1e-6
LNX_EPS = 64e-5
HEAD = 64
CHUNK = 64
LANES = 128
SUBLANES = 8
S5_GROUP = 16
S5_STATE = 64
CONV_WIDTH = 3
MXU_K = 256
VMEM_LIMIT = 56 * 1024 * 1024


def _rms(x, g):
    return x * lax.rsqrt(jnp.mean(x * x, axis=-1, keepdims=True) + NORM_EPS) * g


def _split(x):
    hi = x.astype(BF16)
    lo = (x - hi.astype(F32)).astype(BF16)
    return hi, lo


_NN = (((1,), (0,)), ((), ()))
_NT = (((1,), (1,)), ((), ()))
_TN = (((0,), (0,)), ((), ()))


def _mm(a, b, dims=_NN, passes=1):
    d = lambda p, q: lax.dot_general(p, q, dims, preferred_element_type=F32)
    if passes == 1:
        return d(a.astype(BF16), b.astype(BF16))
    ah, al = _split(a)
    if passes == 2:
        bh = b.astype(BF16)
        return d(ah, bh) + d(al, bh)
    bh, bl = _split(b)
    return d(ah, bh) + (d(ah, bl) + d(al, bh))


def _const_spec(shape):
    nd = len(shape)
    return pl.BlockSpec(shape, lambda *_: (0,) * nd)


def _params(sem):
    return pltpu.CompilerParams(dimension_semantics=sem, vmem_limit_bytes=VMEM_LIMIT)


def _in_proj_body(tm, n_rwkv, n_s5, x_ref, g_ref, w_ref, bg_ref, mu_ref, p_ref, u_ref, gate_ref, carry_ref):
    @pl.when(pl.program_id(1) == 0)
    def _():
        carry_ref[...] = jnp.zeros_like(carry_ref)

    h = _rms(x_ref[0], g_ref[...]).astype(BF16)
    p = jnp.dot(h, w_ref[:, :n_rwkv], preferred_element_type=F32)
    row = lax.broadcasted_iota(jnp.int32, (tm, 1), 0)
    prev = jnp.where(row == 0, carry_ref[...], pltpu.roll(p, 1, 0))
    carry_ref[...] = p[tm - 1:tm, :]
    p_ref[0] = p + (prev - p) * mu_ref[...]
    u_ref[0] = jnp.dot(h, w_ref[:, n_rwkv:n_rwkv + n_s5], preferred_element_type=F32)
    gate_ref[0] = jax.nn.sigmoid(
        jnp.dot(h, w_ref[:, n_rwkv + n_s5:], preferred_element_type=F32) + bg_ref[...]).astype(BF16)


def _in_proj(x3, g, w_in, b_gate, mu, n_rwkv, n_s5, tm):
    b, l, d = x3.shape
    assert l % tm == 0
    n_in = w_in.shape[1]
    n_gate = n_in - n_rwkv - n_s5
    tile = lambda n: pl.BlockSpec((1, tm, n), lambda i, j: (i, j, 0))
    return pl.pallas_call(
        functools.partial(_in_proj_body, tm, n_rwkv, n_s5),
        grid=(b, l // tm),
        in_specs=[tile(d), _const_spec((1, d)), _const_spec((d, n_in)), _const_spec((1, n_gate)),
                  _const_spec((1, n_rwkv))],
        out_specs=[tile(n_rwkv), tile(n_s5), tile(n_gate)],
        out_shape=[jax.ShapeDtypeStruct((b, l, n_rwkv), F32), jax.ShapeDtypeStruct((b, l, n_s5), F32),
                   jax.ShapeDtypeStruct((b, l, n_gate), BF16)],
        scratch_shapes=[pltpu.VMEM((1, n_rwkv), F32)],
        compiler_params=_params(("arbitrary", "arbitrary")),
        name="in_proj",
    )(x3, g, w_in, b_gate, mu)


def _mstack(x, head_masks):
    z = jnp.zeros_like(x)
    return jnp.concatenate([jnp.where(m, x, z) for m in head_masks], axis=0)


def _dot(a, b, dims=_NN):
    return lax.dot_general(a, b, dims, preferred_element_type=F32)


def _cat0(*xs):
    return jnp.concatenate(xs, axis=0)


def _cat1(*xs):
    return jnp.concatenate(xs, axis=1)


def _wkv_chunks(rt, at, kt, bt, bh, kh, v, pc, hs, masks):
    pair_masks, quad_masks, strict, incl, eye4, bd, diag = masks
    c = CHUNK
    ms = lambda x: _mstack(x, pair_masks)
    ms4 = lambda x: _mstack(x, quad_masks)
    b16 = lambda x: x.astype(BF16)
    ids = [(ci, j) for ci in range(len(rt)) for j in range(len(rt[0]))]
    pick = lambda nested: [nested[ci][j] for ci, j in ids]
    rt32 = pick(rt)
    rt, at, kt, bt, bh, kh, v = [[b16(x) for x in pick(n)] for n in (rt, at, kt, bt, bh, kh, v)]
    n = range(len(ids))
    msv = [ms(x) for x in v]
    g = [_dot(_cat0(at[i], rt[i]), _cat0(ms(bt[i]), ms(kt[i])), _NT) for i in n]
    a_ab = [jnp.where(strict, x[:c, :LANES], 0.0) for x in g]
    a_ak = [b16(jnp.where(strict, x[:c, LANES:], 0.0)) for x in g]
    a_rbk = [b16(_cat1(jnp.where(incl, x[c:, :LANES], 0.0), jnp.where(incl, x[c:, LANES:], 0.0))) for x in g]
    nq = range(len(ids) // 2)
    aq = [_cat1(a_ab[2 * k], a_ab[2 * k + 1]) for k in nq]
    aq16 = [b16(x) for x in aq]
    q = [b16(_dot(aq16[k], ms4(aq16[k]))) for k in nq]
    t_inv = [eye4 + x for x in aq]
    n_sq = int(math.log2(c)) - 1
    for s in range(n_sq - 1):
        tq = [_dot(_cat0(b16(t_inv[k]), q[k]), ms4(q[k])) for k in nq]
        t_inv = [t_inv[k] + tq[k][:c] for k in nq]
        q = [b16(tq[k][c:]) for k in nq]
    t16 = [b16(t_inv[k]) for k in nq]
    t_inv = [b16(t_inv[k] + _dot(t16[k], ms4(q[k]))) for k in nq]
    t_inv = [t_inv[i // 2][:, (i % 2) * LANES:(i % 2 + 1) * LANES] for i in n]
    w0 = [b16(_dot(a_ak[i], msv[i])) for i in n]
    x1 = [_dot(t_inv[i], _cat1(ms(at[i]), ms(w0[i]))) for i in n]
    ap = [b16(x[:, :LANES]) for x in x1]
    u0 = [b16(x[:, LANES:]) for x in x1]
    zero = jnp.zeros((2 * c, LANES), BF16)
    ry = [_dot(a_rbk[i], _cat0(_cat1(ms(ap[i]), ms(u0[i])), _cat1(zero, msv[i]))) for i in n]
    mh = [_dot(_cat0(bh[i], kh[i]), _cat0(_cat1(ap[i], u0[i]), _cat1(zero[:c], v[i])), _TN) for i in n]
    rp = [b16(rt32[i] + ry[i][:, :LANES]) for i in n]
    ys, hs = [], list(hs)
    for i, (ci, j) in enumerate(ids):
        m = jnp.where(bd, mh[i][:, :LANES], 0.0) + jnp.where(diag, pc[ci][j], 0.0)
        yh = _dot(_cat0(rp[i], b16(m)), b16(hs[j]))
        ys.append(yh[:c] + ry[i][:, LANES:])
        hs[j] = yh[c:] + jnp.where(bd, mh[i][:, LANES:], 0.0)
    n_pairs = len(hs)
    return [ys[ci * n_pairs:(ci + 1) * n_pairs] for ci in range(len(ids) // n_pairs)], hs


def _split3(x):
    hi = x.astype(BF16)
    r1 = x - hi.astype(F32)
    mid = r1.astype(BF16)
    return hi, mid, (r1 - mid.astype(F32)).astype(BF16)


def _rwkv_body(tl, width, p_ref, w0_ref, wwa_ref, a0_ref, g2_ref, kk_ref, ka_ref, rk_ref,
               lnw_ref, lnb_ref, ones_ref, y_ref, h_ref, pre_ref, yacc_ref):
    @pl.when(pl.program_id(1) == 0)
    def _():
        h_ref[...] = jnp.zeros_like(h_ref)

    w3 = 3 * width
    r_cols, v_cols = slice(0, width), slice(2 * width, w3)
    r = p_ref[0, :, r_cols]
    k = p_ref[0, :, width:2 * width]
    v = p_ref[0, :, v_cols]
    wa = p_ref[0, :, w3:w3 + LANES]
    gd = p_ref[0, :, w3 + LANES:]
    lane = lax.broadcasted_iota(jnp.int32, (1, LANES), 1)
    m0 = lane < HEAD
    twa = _mm(jnp.where(m0, jnp.tanh(wa), wa), wwa_ref[...])
    lw = -math.exp(-0.5) * jax.nn.sigmoid(w0_ref[...] + twa[:, :width])
    a = jax.nn.sigmoid(a0_ref[...] + twa[:, width:])
    g = _mm(jax.nn.sigmoid(gd), g2_ref[...])
    ones = ones_ref[...]
    nk = ones.shape[0]
    segsum = lambda x: _cat1(*[_mm(x[:, i:i + nk], ones) for i in range(0, width, nk)])
    kk = k * kk_ref[...]
    kk = kk * lax.rsqrt(jnp.maximum(segsum(kk * kk), 1e-24))
    k2 = k * (1.0 + (a - 1.0) * ka_ref[...])
    bonus = segsum(r * k2 * rk_ref[...]) * v
    pre_ref[0] = k2
    pre_ref[1] = -kk
    pre_ref[2] = kk * a
    pre_ref[3] = lw

    c = CHUNK
    rowc = lax.broadcasted_iota(jnp.int32, (c, LANES), 0)
    colc = lax.broadcasted_iota(jnp.int32, (c, LANES), 1) % HEAD
    r2 = lax.broadcasted_iota(jnp.int32, (LANES, LANES), 0)
    c2 = lax.broadcasted_iota(jnp.int32, (LANES, LANES), 1)
    lane4 = lax.broadcasted_iota(jnp.int32, (1, 2 * LANES), 1) // HEAD
    col4 = lax.broadcasted_iota(jnp.int32, (c, 2 * LANES), 1) % HEAD
    row4 = lax.broadcasted_iota(jnp.int32, (c, 2 * LANES), 0)
    masks = ([m0, jnp.logical_not(m0)], [lane4 == i for i in range(2 * LANES // HEAD)], colc < rowc, colc <= rowc,
             (col4 == row4).astype(F32), (r2 // HEAD) == (c2 // HEAD), r2 == c2)
    ltri = (lax.broadcasted_iota(jnp.int32, (c, c), 1) <= lax.broadcasted_iota(jnp.int32, (c, c), 0)).astype(BF16)
    ltri2 = _cat1(ltri, ltri)
    n_pairs = width // LANES
    names = ("rt", "at", "kt", "bt", "bh", "kh", "v", "pc")
    ops = {nm: [] for nm in names}
    for ci in range(tl // c):
        rows = slice(ci * c, (ci + 1) * c)
        lwc = pre_ref[3, rows, :]
        cum = _dot(ltri2, _cat0(*_split(lwc)))
        pw = jnp.exp(cum)
        pinv = jnp.exp(-cum)
        kt = pre_ref[0, rows, :] * pinv
        bt = pre_ref[2, rows, :] * pinv
        pc = pw[c - 1:c, :]
        full = {"rt": p_ref[0, rows, r_cols] * pw, "at": pre_ref[1, rows, :] * jnp.exp(cum - lwc), "kt": kt,
                "bt": bt, "bh": bt * pc, "kh": kt * pc, "v": p_ref[0, rows, v_cols], "pc": pc}
        for nm in names:
            ops[nm].append([full[nm][:, j * LANES:(j + 1) * LANES] for j in range(n_pairs)])
    ys, hs = _wkv_chunks(*[ops[nm] for nm in names], [h_ref[j] for j in range(n_pairs)], masks)
    for ci, yrow in enumerate(ys):
        for j, y in enumerate(yrow):
            yacc_ref[ci * c:(ci + 1) * c, j * LANES:(j + 1) * LANES] = y
    for j in range(n_pairs):
        h_ref[j] = hs[j]

    y = yacc_ref[...]
    mean = segsum(y) * (1.0 / HEAD)
    yc = y - mean
    var = segsum(yc * yc) * (1.0 / HEAD)
    yn = yc * lax.rsqrt(var + LNX_EPS) * lnw_ref[...] + lnb_ref[...]
    y_ref[0] = ((yn + bonus) * g).astype(BF16)


def _rwkv(p3, w0, wwa, a0, g2, k_k, k_a, r_k, lnx_w, lnx_b, ones_bd, width, tl):
    b, l, n_rwkv = p3.shape
    assert l % tl == 0 and tl % CHUNK == 0
    n_pairs = width // LANES
    consts = [w0, wwa, a0, g2, k_k, k_a, r_k, lnx_w, lnx_b, ones_bd]
    return pl.pallas_call(
        functools.partial(_rwkv_body, tl, width),
        grid=(b, l // tl),
        in_specs=[pl.BlockSpec((1, tl, n_rwkv), lambda i, j: (i, j, 0))] + [_const_spec(c.shape) for c in consts],
        out_specs=pl.BlockSpec((1, tl, width), lambda i, j: (i, j, 0)),
        out_shape=jax.ShapeDtypeStruct((b, l, width), BF16),
        scratch_shapes=[pltpu.VMEM((n_pairs, LANES, LANES), F32), pltpu.VMEM((4, tl, width), F32),
                        pltpu.VMEM((tl, width), F32)],
        compiler_params=_params(("arbitrary", "arbitrary")),
        name="rwkv7",
    )(p3, *consts)


def _s5_disc_body(are_ref, aim_ref, dt_ref, bre_ref, bim_ref, abr_ref, abi_ref, bbr_ref, bbi_ref):
    lre, lim, dt = are_ref[...], aim_ref[...], jnp.exp(dt_ref[...])
    mag = jnp.exp(lre * dt)
    abr = mag * jnp.cos(lim * dt)
    abi = mag * jnp.sin(lim * dt)
    inv = 1.0 / (lre * lre + lim * lim)
    nr, ni = abr - 1.0, abi
    cr = (nr * lre + ni * lim) * inv
    ci = (ni * lre - nr * lim) * inv
    bre, bim = bre_ref[...], bim_ref[...]
    abr_ref[...] = abr
    abi_ref[...] = abi
    bbr_ref[...] = cr * bre - ci * bim
    bbi_ref[...] = cr * bim + ci * bre


def _s5_disc(a_re, a_im, log_step, b_re, b_im):
    g, p, c = b_re.shape
    col = lambda x: x.reshape(g * p, 1)
    dt = jnp.broadcast_to(log_step[:, None], (g, p))
    outs = pl.pallas_call(
        _s5_disc_body,
        out_shape=[jax.ShapeDtypeStruct((g * p, 1), F32)] * 2 + [jax.ShapeDtypeStruct((g * p, c), F32)] * 2,
        name="s5_disc",
    )(col(a_re), col(a_im), col(dt), b_re.reshape(g * p, c), b_im.reshape(g * p, c))
    abr, abi, bbr, bbi = outs
    return abr.reshape(g, p), abi.reshape(g, p), bbr.reshape(g, p, c), bbi.reshape(g, p, c)


def _s5_body(tt, nh, col_w, u_ref, perm_ref, permt_ref, b_ref, are_ref, aim_ref, c_ref, d_ref, wglu_ref, bglu_ref,
             y_ref, st_ref, s_ref):
    @pl.when(pl.program_id(0) == 0)
    def _():
        s_ref[...] = jnp.zeros_like(s_ref)

    nb = SUBLANES
    rows = nb * tt
    width = u_ref.shape[-1]
    kin = width // nh
    nsh = are_ref.shape[-1]
    u = u_ref[...].reshape(rows, width)
    u_tm = jnp.dot(perm_ref[...], u.astype(BF16), preferred_element_type=F32).astype(BF16)
    for h in range(nh):
        st_ref[h] = jnp.dot(u_tm[:, h * kin:(h + 1) * kin], b_ref[h], preferred_element_type=F32)
    for h in range(nh):
        for cg in range(nsh // col_w):
            re_c = slice(cg * col_w, (cg + 1) * col_w)
            im_c = slice(nsh + cg * col_w, nsh + (cg + 1) * col_w)
            are = jnp.broadcast_to(are_ref[h, :, re_c], (nb, col_w))
            aim = jnp.broadcast_to(aim_ref[h, :, re_c], (nb, col_w))

            def step(t, carry):
                sre, sim = carry
                r8 = pl.ds(pl.multiple_of(t * nb, nb), nb)
                nre = are * sre - aim * sim + st_ref[h, r8, re_c]
                nim = are * sim + aim * sre + st_ref[h, r8, im_c]
                st_ref[h, r8, re_c] = nre
                st_ref[h, r8, im_c] = nim
                return nre, nim

            sre, sim = lax.fori_loop(0, tt, step, (s_ref[h, :, re_c], s_ref[h, :, im_c]), unroll=True)
            s_ref[h, :, re_c] = sre
            s_ref[h, :, im_c] = sim
    y_tm = jnp.concatenate([jnp.dot(st_ref[h].astype(BF16), c_ref[h], preferred_element_type=F32)
                            for h in range(nh)], axis=1)
    permt = permt_ref[...]
    y = _dot(_cat1(permt, permt), _cat0(*_split(y_tm)))
    y = jax.nn.gelu(y + d_ref[...] * u)
    y = y * jax.nn.sigmoid(jnp.dot(y.astype(BF16), wglu_ref[...], preferred_element_type=F32) + bglu_ref[...])
    y_ref[...] = y.astype(BF16).reshape(nb, tt, width)


def _s5(u3, perm, b_h, are, aim, c_h, d, w_glu, b_glu, tt):
    b, l, width = u3.shape
    assert l % tt == 0 and b == SUBLANES
    nh, _, nsh2 = b_h.shape
    consts = [perm, perm.T, b_h, are, aim, c_h, d, w_glu, b_glu]
    blk = pl.BlockSpec((b, tt, width), lambda i: (0, i, 0))
    return pl.pallas_call(
        functools.partial(_s5_body, tt, nh, 4 * LANES),
        grid=(l // tt,),
        in_specs=[blk] + [_const_spec(c.shape) for c in consts],
        out_specs=blk,
        out_shape=jax.ShapeDtypeStruct((b, l, width), BF16),
        scratch_shapes=[pltpu.VMEM((nh, b * tt, nsh2), F32), pltpu.VMEM((nh, b, nsh2), F32)],
        compiler_params=_params(("arbitrary",)),
        name="s5",
    )(u3, *consts)


def _mix_ffn_body(tm, d, f, cb, x_ref, yr_ref, ys_ref, gate_ref, wbr_ref, wbs_ref, wout_ref, gmix_ref,
                  gpre_ref, wup_ref, cw_ref, cbias_ref, wdown_ref, gpost_ref, o_ref, carry_ref, act_ref):
    @pl.when(pl.program_id(1) == 0)
    def _():
        carry_ref[...] = jnp.zeros_like(carry_ref)

    o_r = jnp.dot(yr_ref[0], wbr_ref[...], preferred_element_type=F32)
    o_s = jnp.dot(ys_ref[0], wbs_ref[...], preferred_element_type=F32)
    mixed = gate_ref[0, :, :d] * o_r + gate_ref[0, :, d:] * o_s
    mixed = jnp.dot(mixed.astype(BF16), wout_ref[...], preferred_element_type=F32)
    x = x_ref[0] + _rms(mixed, gmix_ref[...])

    h = _rms(x, gpre_ref[...]).astype(BF16)
    nb = SUBLANES
    sub = lax.broadcasted_iota(jnp.int32, (nb, 1), 0)

    def conv(cols):
        z = jnp.dot(h, wup_ref[:, cols], preferred_element_type=F32)
        tail = carry_ref[:, cols]
        carry_ref[:, cols] = z[tm - nb:, :]
        out = cbias_ref[:, cols] + cw_ref[CONV_WIDTH - 1:CONV_WIDTH, cols] * z
        for s in range(1, CONV_WIDTH):
            zr = pltpu.roll(z, s, 0)
            top = jnp.where(sub < s, pltpu.roll(tail, s, 0), zr[:nb])
            zs = jnp.concatenate([top, zr[nb:]], axis=0)
            out = out + cw_ref[CONV_WIDTH - 1 - s:CONV_WIDTH - s, cols] * zs
        return out

    for j in range(f // cb):
        gate = conv(slice(j * cb, (j + 1) * cb))
        val = conv(slice(f + j * cb, f + (j + 1) * cb))
        act_ref[:, j * cb:(j + 1) * cb] = (jax.nn.gelu(gate) * val).astype(BF16)
    y = jnp.dot(act_ref[...], wdown_ref[...], preferred_element_type=F32)
    o_ref[0] = x + _rms(y, gpost_ref[...])


def _mix_ffn(x3, yr, ys, gates, wbr, wbs, wout, g_mix, g_pre, w_up, conv_w, conv_b, w_down, g_post, tm):
    b, l, d = x3.shape
    f = w_down.shape[0]
    cb = 2 * LANES
    assert l % tm == 0 and f % cb == 0
    consts = [wbr, wbs, wout, g_mix, g_pre, w_up, conv_w, conv_b, w_down, g_post]
    tile = lambda a: pl.BlockSpec((1, tm, a.shape[-1]), lambda i, j: (i, j, 0))
    acts = [x3, yr, ys, gates]
    return pl.pallas_call(
        functools.partial(_mix_ffn_body, tm, d, f, cb),
        grid=(b, l // tm),
        in_specs=[tile(a) for a in acts] + [_const_spec(c.shape) for c in consts],
        out_specs=tile(x3),
        out_shape=jax.ShapeDtypeStruct((b, l, d), F32),
        scratch_shapes=[pltpu.VMEM((SUBLANES, 2 * f), F32), pltpu.VMEM((tm, f), BF16)],
        compiler_params=_params(("arbitrary", "arbitrary")),
        name="mix_ffn",
    )(*acts, *consts)


def _block_diag(blocks):
    g, r, c = blocks.shape
    eye = jnp.eye(g, dtype=blocks.dtype)
    return (blocks[:, :, None, :] * eye[:, None, :, None]).reshape(g * r, g * c)


def _layer(x, norm_mix_pre, norm_mix_post, norm_ffn_pre, norm_ffn_post, w_in, b_gate, mu, w0, w2, a0, a2, g2,
           k_k, k_a, r_k, lnx_w, lnx_b, s5_a_re, s5_a_im, s5_b_re, s5_b_im, s5_c_re, s5_c_im, s5_d, s5_log_step,
           s5_w_glu, s5_b_glu, w_branch_rwkv, w_branch_s5, w_out, ffn_w_up, ffn_conv_w, ffn_conv_b, ffn_w_down,
           tiles):
    b, l, d = x.shape
    t = b * l
    width = w0.shape[0]
    s5_width = s5_d.shape[0]
    n_rwkv = mu.shape[0]
    rank_w, rank_a = w2.shape[0], a2.shape[0]
    assert rank_w == HEAD and rank_a == HEAD and g2.shape[0] == LANES and b == SUBLANES
    rowv = lambda vec: vec.reshape(1, -1)
    p, u, gates = _in_proj(x, rowv(norm_mix_pre), w_in.astype(BF16), rowv(b_gate), rowv(mu), n_rwkv, s5_width,
                           tiles["in"])

    wwa = jnp.zeros((LANES, 2 * width), F32).at[:rank_w, :width].set(w2).at[rank_w:, width:].set(a2)
    ones_bd = _block_diag(jnp.ones((MXU_K // HEAD, HEAD, HEAD), F32)).astype(BF16)
    y_rwkv = _rwkv(p, rowv(w0), wwa.astype(BF16), rowv(a0), g2.astype(BF16), rowv(k_k), rowv(k_a), rowv(r_k),
                   rowv(lnx_w), rowv(lnx_b), ones_bd, width, tiles["rwkv"])

    abr, abi, bbr, bbi = _s5_disc(s5_a_re, s5_a_im, s5_log_step, s5_b_re, s5_b_im)
    n_grp = s5_a_re.shape[0]
    gh = MXU_K // S5_GROUP
    nh = n_grp // gh
    halves = lambda x: x.reshape((nh, gh) + x.shape[1:])
    bd = jax.vmap(_block_diag)
    tr = lambda x: halves(x).transpose(0, 1, 3, 2)
    b_h = jnp.concatenate([bd(tr(bbr)), bd(tr(bbi))], axis=2)
    c_h = jnp.concatenate([bd(tr(s5_c_re)), -bd(tr(s5_c_im))], axis=1)
    tt = tiles["s5"]
    ridx = jnp.arange(b * tt)
    perm = (ridx[None, :] == ((ridx % b) * tt + ridx // b)[:, None]).astype(BF16)
    y_s5 = _s5(u, perm, b_h.astype(BF16), abr.reshape(nh, 1, -1), abi.reshape(nh, 1, -1),
               c_h.astype(BF16), rowv(s5_d), s5_w_glu.astype(BF16), rowv(s5_b_glu), tt)

    return _mix_ffn(x, y_rwkv, y_s5, gates, w_branch_rwkv.astype(BF16),
                    w_branch_s5.astype(BF16), w_out.astype(BF16), rowv(norm_mix_post), rowv(norm_ffn_pre),
                    ffn_w_up.astype(BF16), ffn_conv_w, rowv(ffn_conv_b), ffn_w_down.astype(BF16),
                    rowv(norm_ffn_post), tiles["ffn"])


TILES = {"in": 512, "rwkv": 256, "s5": 32, "ffn": 512}


def kernel(x, norm_mix_pre, norm_mix_post, norm_ffn_pre, norm_ffn_post, w_in, b_gate, rwkv_shift_mu, rwkv_w0, rwkv_w2, rwkv_a0, rwkv_a2, rwkv_g2, rwkv_k_k, rwkv_k_a, rwkv_r_k, rwkv_lnx_w, rwkv_lnx_b, s5_a_re, s5_a_im, s5_b_re, s5_b_im, s5_c_re, s5_c_im, s5_d, s5_log_step, s5_w_glu, s5_b_glu, w_branch_rwkv, w_branch_s5, w_out, ffn_w_up, ffn_conv_w, ffn_conv_b, ffn_w_down):
    depth = w_in.shape[0]
    for i in range(depth):
        x = _layer(x, norm_mix_pre[i], norm_mix_post[i], norm_ffn_pre[i], norm_ffn_post[i], w_in[i], b_gate[i],
                   rwkv_shift_mu[i], rwkv_w0[i], rwkv_w2[i], rwkv_a0[i], rwkv_a2[i], rwkv_g2[i], rwkv_k_k[i],
                   rwkv_k_a[i], rwkv_r_k[i].reshape(-1), rwkv_lnx_w[i], rwkv_lnx_b[i], s5_a_re[i], s5_a_im[i],
                   s5_b_re[i], s5_b_im[i], s5_c_re[i], s5_c_im[i], s5_d[i], s5_log_step[i], s5_w_glu[i],
                   s5_b_glu[i], w_branch_rwkv[i], w_branch_s5[i], w_out[i], ffn_w_up[i], ffn_conv_w[i],
                   ffn_conv_b[i], ffn_w_down[i], TILES)
    return x
```

```python
import functools
import math

import jax
import jax.numpy as jnp
from jax import lax
from jax.experimental import pallas as pl
from jax.experimental.pallas import tpu as pltpu

F32 = jnp.float32
BF16 = jnp.bfloat16

NORM_EPS = 1e-6
LNX_EPS = 64e-5
HEAD = 64
CHUNK = 64
WKV_GROUP = 4
LANES = 128
SUBLANES = 8
S5_GROUP = 16
S5_STATE = 64
CONV_WIDTH = 3
MXU_K = 256
VMEM_LIMIT = 56 * 1024 * 1024


def _rms(x, g):
    return x * lax.rsqrt(jnp.mean(x * x, axis=-1, keepdims=True) + NORM_EPS) * g


def _split(x):
    hi = x.astype(BF16)
    lo = (x - hi.astype(F32)).astype(BF16)
    return hi, lo


_NN = (((1,), (0,)), ((), ()))
_NT = (((1,), (1,)), ((), ()))
_TN = (((0,), (0,)), ((), ()))


def _dot(a, b, dims=_NN):
    return lax.dot_general(a, b, dims, preferred_element_type=F32)


def _mm(a, b):
    return _dot(a.astype(BF16), b.astype(BF16))


def _cat0(*xs):
    return jnp.concatenate(xs, axis=0)


def _cat1(*xs):
    return jnp.concatenate(xs, axis=1)


def _const_spec(shape):
    nd = len(shape)
    return pl.BlockSpec(shape, lambda *_: (0,) * nd)


def _params(sem):
    return pltpu.CompilerParams(dimension_semantics=sem, vmem_limit_bytes=VMEM_LIMIT)


def _in_proj_body(tm, n_rwkv, n_s5, x_ref, g_ref, w_ref, bg_ref, mu_ref, p_ref, u_ref, gate_ref, carry_ref):
    @pl.when(pl.program_id(1) == 0)
    def _():
        carry_ref[...] = jnp.zeros_like(carry_ref)

    h = _rms(x_ref[0], g_ref[...]).astype(BF16)
    p = jnp.dot(h, w_ref[:, :n_rwkv], preferred_element_type=F32)
    row = lax.broadcasted_iota(jnp.int32, (tm, 1), 0)
    prev = jnp.where(row == 0, carry_ref[...], pltpu.roll(p, 1, 0))
    carry_ref[...] = p[tm - 1:tm, :]
    p_ref[0] = p + (prev - p) * mu_ref[...]
    u_ref[0] = jnp.dot(h, w_ref[:, n_rwkv:n_rwkv + n_s5], preferred_element_type=F32)
    gate_ref[0] = jax.nn.sigmoid(
        jnp.dot(h, w_ref[:, n_rwkv + n_s5:], preferred_element_type=F32) + bg_ref[...]).astype(BF16)


def _in_proj(x3, g, w_in, b_gate, mu, n_rwkv, n_s5, tm):
    b, l, d = x3.shape
    assert l % tm == 0
    n_in = w_in.shape[1]
    n_gate = n_in - n_rwkv - n_s5
    tile = lambda n: pl.BlockSpec((1, tm, n), lambda i, j: (i, j, 0))
    return pl.pallas_call(
        functools.partial(_in_proj_body, tm, n_rwkv, n_s5),
        grid=(b, l // tm),
        in_specs=[tile(d), _const_spec((1, d)), _const_spec((d, n_in)), _const_spec((1, n_gate)),
                  _const_spec((1, n_rwkv))],
        out_specs=[tile(n_rwkv), tile(n_s5), tile(n_gate)],
        out_shape=[jax.ShapeDtypeStruct((b, l, n_rwkv), F32), jax.ShapeDtypeStruct((b, l, n_s5), F32),
                   jax.ShapeDtypeStruct((b, l, n_gate), BF16)],
        scratch_shapes=[pltpu.VMEM((1, n_rwkv), F32)],
        compiler_params=_params(("arbitrary", "arbitrary")),
        name="in_proj",
    )(x3, g, w_in, b_gate, mu)


def _mstack(x, head_masks):
    z = jnp.zeros_like(x)
    return jnp.concatenate([jnp.where(m, x, z) for m in head_masks], axis=0)


def _wkv_stages(fetch, chunk_ids, hs, masks):
    pair_masks, strict, incl, eye, bd, diag = masks
    c = CHUNK
    ms = lambda x: _mstack(x, pair_masks)
    b16 = lambda x: x.astype(BF16)
    ops = [fetch(ci) for ci in chunk_ids]
    n_pairs = len(hs)
    ids = [(k, j) for k in range(len(chunk_ids)) for j in range(n_pairs)]
    get = lambda name: [ops[k][name][j] for k, j in ids]
    rt32, at, kt, bt, bh, kh, v, pc = [get(name) for name in ("rt", "at", "kt", "bt", "bh", "kh", "v", "pc")]
    rt = [b16(x) for x in rt32]
    n = range(len(ids))
    msv = [ms(x) for x in v]
    g = [_dot(_cat0(at[i], rt[i]), _cat0(ms(bt[i]), ms(kt[i])), _NT) for i in n]
    yield
    a_ab = [jnp.where(strict, x[:c, :LANES], 0.0) for x in g]
    a_ak = [b16(jnp.where(strict, x[:c, LANES:], 0.0)) for x in g]
    a_rbk = [b16(_cat1(jnp.where(incl, x[c:, :LANES], 0.0), jnp.where(incl, x[c:, LANES:], 0.0))) for x in g]
    ab16 = [b16(x) for x in a_ab]
    q = [b16(_dot(ab16[i], ms(ab16[i]))) for i in n]
    t_inv = [eye + x for x in a_ab]
    yield
    n_sq = int(math.log2(c)) - 1
    for s in range(n_sq - 1):
        tq = [_dot(_cat0(b16(t_inv[i]), q[i]), ms(q[i])) for i in n]
        t_inv = [t_inv[i] + tq[i][:c] for i in n]
        q = [b16(tq[i][c:]) for i in n]
        yield
    t16 = [b16(t_inv[i]) for i in n]
    t_inv = [b16(t_inv[i] + _dot(t16[i], ms(q[i]))) for i in n]
    w0 = [b16(_dot(a_ak[i], msv[i])) for i in n]
    yield
    x1 = [_dot(t_inv[i], _cat1(ms(at[i]), ms(w0[i]))) for i in n]
    ap = [b16(x[:, :LANES]) for x in x1]
    u0 = [b16(x[:, LANES:]) for x in x1]
    yield
    zero = jnp.zeros((2 * c, LANES), BF16)
    ry = [_dot(a_rbk[i], _cat0(_cat1(ms(ap[i]), ms(u0[i])), _cat1(zero, msv[i]))) for i in n]
    yield
    mh = [_dot(_cat0(bh[i], kh[i]), _cat0(_cat1(ap[i], u0[i]), _cat1(zero[:c], v[i])), _TN) for i in n]
    rp = [b16(rt32[i] + ry[i][:, :LANES]) for i in n]
    yield
    ys, hs = [], list(hs)
    for i, (k, j) in enumerate(ids):
        m = jnp.where(bd, mh[i][:, :LANES], 0.0) + jnp.where(diag, pc[i], 0.0)
        yh = _dot(_cat0(rp[i], b16(m)), b16(hs[j]))
        ys.append(yh[:c] + ry[i][:, LANES:])
        hs[j] = yh[c:] + jnp.where(bd, mh[i][:, LANES:], 0.0)
        if j == n_pairs - 1:
            yield
    return ys, hs


def _drive(stages, fillers, per_stage):
    while True:
        try:
            next(stages)
        except StopIteration as done:
            return done.value
        for _ in range(per_stage):
            if fillers:
                fillers.pop(0)()


def _rwkv_body(tl, width, p_ref, w0_ref, wwa_ref, a0_ref, g2_ref, kk_ref, ka_ref, rk_ref, lnw_ref, lnb_ref,
               ones_ref, y_ref, h_ref, pre_ref, twa_ref, g_ref, bonus_ref, ss_ref, rt_ref, ops_ref, pc_ref,
               yacc_ref):
    @pl.when(pl.program_id(1) == 0)
    def _():
        h_ref[...] = jnp.zeros_like(h_ref)

    c = CHUNK
    w3 = 3 * width
    n_pairs = width // LANES
    r_cols, k_cols, v_cols = slice(0, width), slice(width, 2 * width), slice(2 * width, w3)
    lane = lax.broadcasted_iota(jnp.int32, (1, LANES), 1)
    m0 = lane < HEAD
    ones = ones_ref[...]
    nk = ones.shape[0]
    segsum = lambda x: _cat1(*[_mm(x[:, i:i + nk], ones) for i in range(0, width, nk)])
    rowc = lax.broadcasted_iota(jnp.int32, (c, LANES), 0)
    colc = lax.broadcasted_iota(jnp.int32, (c, LANES), 1) % HEAD
    r2 = lax.broadcasted_iota(jnp.int32, (LANES, LANES), 0)
    c2 = lax.broadcasted_iota(jnp.int32, (LANES, LANES), 1)
    masks = ([m0, jnp.logical_not(m0)], colc < rowc, colc <= rowc, (colc == rowc).astype(F32),
             (r2 // HEAD) == (c2 // HEAD), r2 == c2)
    ltri = (lax.broadcasted_iota(jnp.int32, (c, c), 1) <= lax.broadcasted_iota(jnp.int32, (c, c), 0)).astype(BF16)
    ltri2 = _cat1(ltri, ltri)

    def low_rank(rows):
        wa = p_ref[0, rows, w3:w3 + LANES]
        twa_ref[rows, :] = _mm(jnp.where(m0, jnp.tanh(wa), wa), wwa_ref[...])
        g_ref[rows, :] = _mm(jax.nn.sigmoid(p_ref[0, rows, w3 + LANES:]), g2_ref[...])

    def elementwise(rows):
        pre_ref[3, rows, :] = -math.exp(-0.5) * jax.nn.sigmoid(w0_ref[...] + twa_ref[rows, :width])
        a = jax.nn.sigmoid(a0_ref[...] + twa_ref[rows, width:])
        k = p_ref[0, rows, k_cols]
        k2 = k * (1.0 + (a - 1.0) * ka_ref[...])
        pre_ref[0, rows, :] = k2
        pre_ref[1, rows, :] = k * kk_ref[...]
        pre_ref[2, rows, :] = a
        bonus_ref[rows, :] = p_ref[0, rows, r_cols] * k2 * rk_ref[...]

    def head_sums(rows):
        kk = pre_ref[1, rows, :]
        ss_ref[rows, :] = segsum(kk * kk)
        bonus_ref[rows, :] = segsum(bonus_ref[rows, :])

    def normalise(rows):
        kk = pre_ref[1, rows, :] * lax.rsqrt(jnp.maximum(ss_ref[rows, :], 1e-24))
        pre_ref[2, rows, :] = kk * pre_ref[2, rows, :]
        pre_ref[1, rows, :] = -kk
        bonus_ref[rows, :] = bonus_ref[rows, :] * p_ref[0, rows, v_cols]

    def decay_scale(ci):
        rows = slice(ci * c, (ci + 1) * c)
        lwc = pre_ref[3, rows, :]
        cum = _dot(ltri2, _cat0(*_split(lwc)))
        pw = jnp.exp(cum)
        pinv = jnp.exp(-cum)
        kt = pre_ref[0, rows, :] * pinv
        bt = pre_ref[2, rows, :] * pinv
        pc = pw[c - 1:c, :]
        rt_ref[rows, :] = p_ref[0, rows, r_cols] * pw
        for i, x in enumerate((pre_ref[1, rows, :] * jnp.exp(cum - lwc), kt, bt, bt * pc, kt * pc,
                               p_ref[0, rows, v_cols])):
            ops_ref[i, rows, :] = x.astype(BF16)
        pc_ref[ci:ci + 1, :] = pc

    def fetch(ci):
        rows = slice(ci * c, (ci + 1) * c)
        tiles = lambda x: [x[:, j * LANES:(j + 1) * LANES] for j in range(n_pairs)]
        out = {name: tiles(ops_ref[i, rows, :]) for i, name in enumerate(("at", "kt", "bt", "bh", "kh", "v"))}
        out["rt"] = tiles(rt_ref[rows, :])
        out["pc"] = tiles(pc_ref[ci:ci + 1, :])
        return out

    def centre(rows):
        y = yacc_ref[rows, :]
        yacc_ref[rows, :] = y - segsum(y) * (1.0 / HEAD)

    def group_norm(rows):
        yc = yacc_ref[rows, :]
        var = segsum(yc * yc) * (1.0 / HEAD)
        yn = yc * lax.rsqrt(var + LNX_EPS) * lnw_ref[...] + lnb_ref[...]
        y_ref[0, rows, :] = ((yn + bonus_ref[rows, :]) * g_ref[rows, :]).astype(BF16)

    gc = WKV_GROUP
    n_groups = tl // (c * gc)
    half = c * gc // 2

    def prepare(gi):
        r0 = gi * gc * c
        blocks = [slice(r0 + i * c, r0 + (i + 1) * c) for i in range(gc)]
        halves = [slice(r0, r0 + half), slice(r0 + half, r0 + 2 * half)]
        return ([functools.partial(low_rank, h) for h in halves] + [functools.partial(elementwise, b) for b in blocks]
                + [functools.partial(head_sums, h) for h in halves] + [functools.partial(normalise, b) for b in blocks]
                + [functools.partial(decay_scale, ci) for ci in range(gi * gc, (gi + 1) * gc)])

    def finish(gi):
        r0 = gi * gc * c
        halves = [slice(r0, r0 + half), slice(r0 + half, r0 + 2 * half)]
        return [functools.partial(f, h) for f in (centre, group_norm) for h in halves]

    for thunk in prepare(0):
        thunk()
    hs = [h_ref[j] for j in range(n_pairs)]
    for gi in range(n_groups):
        fillers = (prepare(gi + 1) if gi + 1 < n_groups else []) + (finish(gi - 1) if gi > 0 else [])
        chunk_ids = list(range(gi * gc, (gi + 1) * gc))
        ys, hs = _drive(_wkv_stages(fetch, chunk_ids, hs, masks), fillers, 2)
        for thunk in fillers:
            thunk()
        for i, y in enumerate(ys):
            ci, j = chunk_ids[i // n_pairs], i % n_pairs
            yacc_ref[ci * c:(ci + 1) * c, j * LANES:(j + 1) * LANES] = y
    for j in range(n_pairs):
        h_ref[j] = hs[j]
    for thunk in finish(n_groups - 1):
        thunk()


def _rwkv(p3, w0, wwa, a0, g2, k_k, k_a, r_k, lnx_w, lnx_b, ones_bd, width, tl):
    b, l, n_rwkv = p3.shape
    assert l % tl == 0 and tl % (CHUNK * WKV_GROUP) == 0 and tl // CHUNK <= SUBLANES
    n_pairs = width // LANES
    consts = [w0, wwa, a0, g2, k_k, k_a, r_k, lnx_w, lnx_b, ones_bd]
    rows = lambda n=width: pltpu.VMEM((tl, n), F32)
    return pl.pallas_call(
        functools.partial(_rwkv_body, tl, width),
        grid=(b, l // tl),
        in_specs=[pl.BlockSpec((1, tl, n_rwkv), lambda i, j: (i, j, 0))] + [_const_spec(c.shape) for c in consts],
        out_specs=pl.BlockSpec((1, tl, width), lambda i, j: (i, j, 0)),
        out_shape=jax.ShapeDtypeStruct((b, l, width), BF16),
        scratch_shapes=[pltpu.VMEM((n_pairs, LANES, LANES), F32), pltpu.VMEM((4, tl, width), F32),
                        rows(2 * width), rows(), rows(), rows(), rows(), pltpu.VMEM((6, tl, width), BF16),
                        pltpu.VMEM((SUBLANES, width), F32), rows()],
        compiler_params=_params(("arbitrary", "arbitrary")),
        name="rwkv7",
    )(p3, *consts)


def _s5_disc_body(are_ref, aim_ref, dt_ref, bre_ref, bim_ref, abr_ref, abi_ref, bbr_ref, bbi_ref):
    lre, lim, dt = are_ref[...], aim_ref[...], jnp.exp(dt_ref[...])
    mag = jnp.exp(lre * dt)
    abr = mag * jnp.cos(lim * dt)
    abi = mag * jnp.sin(lim * dt)
    inv = 1.0 / (lre * lre + lim * lim)
    nr, ni = abr - 1.0, abi
    cr = (nr * lre + ni * lim) * inv
    ci = (ni * lre - nr * lim) * inv
    bre, bim = bre_ref[...], bim_ref[...]
    abr_ref[...] = abr
    abi_ref[...] = abi
    bbr_ref[...] = cr * bre - ci * bim
    bbi_ref[...] = cr * bim + ci * bre


def _s5_disc(a_re, a_im, log_step, b_re, b_im):
    g, p, c = b_re.shape
    col = lambda x: x.reshape(g * p, 1)
    dt = jnp.broadcast_to(log_step[:, None], (g, p))
    outs = pl.pallas_call(
        _s5_disc_body,
        out_shape=[jax.ShapeDtypeStruct((g * p, 1), F32)] * 2 + [jax.ShapeDtypeStruct((g * p, c), F32)] * 2,
        name="s5_disc",
    )(col(a_re), col(a_im), col(dt), b_re.reshape(g * p, c), b_im.reshape(g * p, c))
    abr, abi, bbr, bbi = outs
    return abr.reshape(g, p), abi.reshape(g, p), bbr.reshape(g, p, c), bbi.reshape(g, p, c)


def _s5_body(tt, n_sub, nh, col_w, u_ref, perm_ref, permt_ref, b_ref, are_ref, aim_ref, c_ref, d_ref, wglu_ref,
             bglu_ref, y_ref, st_ref, s_ref):
    @pl.when(pl.program_id(0) == 0)
    def _():
        s_ref[...] = jnp.zeros_like(s_ref)

    nb = SUBLANES
    rows = nb * tt
    width = u_ref.shape[-1]
    kin = width // nh
    nsh = are_ref.shape[-1]
    permt = permt_ref[...]
    permt2 = _cat1(permt, permt)
    us = [u_ref[:, i * tt:(i + 1) * tt, :].reshape(rows, width) for i in range(n_sub)]
    for i in range(n_sub):
        u_tm = jnp.dot(perm_ref[...], us[i].astype(BF16), preferred_element_type=F32).astype(BF16)
        for h in range(nh):
            st_ref[i, h] = jnp.dot(u_tm[:, h * kin:(h + 1) * kin], b_ref[h], preferred_element_type=F32)
    for i in range(n_sub):
        for h in range(nh):
            for cg in range(nsh // col_w):
                re_c = slice(cg * col_w, (cg + 1) * col_w)
                im_c = slice(nsh + cg * col_w, nsh + (cg + 1) * col_w)
                are = jnp.broadcast_to(are_ref[h, :, re_c], (nb, col_w))
                aim = jnp.broadcast_to(aim_ref[h, :, re_c], (nb, col_w))

                def step(t, carry):
                    sre, sim = carry
                    r8 = pl.ds(pl.multiple_of(t * nb, nb), nb)
                    nre = are * sre - aim * sim + st_ref[i, h, r8, re_c]
                    nim = are * sim + aim * sre + st_ref[i, h, r8, im_c]
                    st_ref[i, h, r8, re_c] = nre
                    st_ref[i, h, r8, im_c] = nim
                    return nre, nim

                sre, sim = lax.fori_loop(0, tt, step, (s_ref[h, :, re_c], s_ref[h, :, im_c]), unroll=True)
                s_ref[h, :, re_c] = sre
                s_ref[h, :, im_c] = sim
        y_tm = jnp.concatenate([jnp.dot(st_ref[i, h].astype(BF16), c_ref[h], preferred_element_type=F32)
                                for h in range(nh)], axis=1)
        y = _dot(permt2, _cat0(*_split(y_tm)))
        y = jax.nn.gelu(y + d_ref[...] * us[i])
        y = y * jax.nn.sigmoid(jnp.dot(y.astype(BF16), wglu_ref[...], preferred_element_type=F32) + bglu_ref[...])
        y_ref[:, i * tt:(i + 1) * tt, :] = y.astype(BF16).reshape(nb, tt, width)


def _s5(u3, perm, b_h, are, aim, c_h, d, w_glu, b_glu, tt, n_sub):
    b, l, width = u3.shape
    assert l % (tt * n_sub) == 0 and b == SUBLANES
    nh, _, nsh2 = b_h.shape
    consts = [perm, perm.T, b_h, are, aim, c_h, d, w_glu, b_glu]
    blk = pl.BlockSpec((b, tt * n_sub, width), lambda i: (0, i, 0))
    return pl.pallas_call(
        functools.partial(_s5_body, tt, n_sub, nh, 4 * LANES),
        grid=(l // (tt * n_sub),),
        in_specs=[blk] + [_const_spec(c.shape) for c in consts],
        out_specs=blk,
        out_shape=jax.ShapeDtypeStruct((b, l, width), BF16),
        scratch_shapes=[pltpu.VMEM((n_sub, nh, b * tt, nsh2), F32), pltpu.VMEM((nh, b, nsh2), F32)],
        compiler_params=_params(("arbitrary",)),
        name="s5",
    )(u3, *consts)


def _mix_ffn_body(tm, d, f, cb, x_ref, yr_ref, ys_ref, gate_ref, wbr_ref, wbs_ref, wout_ref, gmix_ref,
                  gpre_ref, wup_ref, cw_ref, cbias_ref, wdown_ref, gpost_ref, o_ref, carry_ref, act_ref):
    @pl.when(pl.program_id(1) == 0)
    def _():
        carry_ref[...] = jnp.zeros_like(carry_ref)

    o_r = jnp.dot(yr_ref[0], wbr_ref[...], preferred_element_type=F32)
    o_s = jnp.dot(ys_ref[0], wbs_ref[...], preferred_element_type=F32)
    mixed = gate_ref[0, :, :d] * o_r + gate_ref[0, :, d:] * o_s
    mixed = jnp.dot(mixed.astype(BF16), wout_ref[...], preferred_element_type=F32)
    x = x_ref[0] + _rms(mixed, gmix_ref[...])

    h = _rms(x, gpre_ref[...]).astype(BF16)
    nb = SUBLANES
    sub = lax.broadcasted_iota(jnp.int32, (nb, 1), 0)

    def conv(cols):
        z = jnp.dot(h, wup_ref[:, cols], preferred_element_type=F32)
        tail = carry_ref[:, cols]
        carry_ref[:, cols] = z[tm - nb:, :]
        out = cbias_ref[:, cols] + cw_ref[CONV_WIDTH - 1:CONV_WIDTH, cols] * z
        for s in range(1, CONV_WIDTH):
            zr = pltpu.roll(z, s, 0)
            top = jnp.where(sub < s, pltpu.roll(tail, s, 0), zr[:nb])
            zs = jnp.concatenate([top, zr[nb:]], axis=0)
            out = out + cw_ref[CONV_WIDTH - 1 - s:CONV_WIDTH - s, cols] * zs
        return out

    for j in range(f // cb):
        gate = conv(slice(j * cb, (j + 1) * cb))
        val = conv(slice(f + j * cb, f + (j + 1) * cb))
        act_ref[:, j * cb:(j + 1) * cb] = (jax.nn.gelu(gate) * val).astype(BF16)
    y = jnp.dot(act_ref[...], wdown_ref[...], preferred_element_type=F32)
    o_ref[0] = x + _rms(y, gpost_ref[...])


def _mix_ffn(x3, yr, ys, gates, wbr, wbs, wout, g_mix, g_pre, w_up, conv_w, conv_b, w_down, g_post, tm):
    b, l, d = x3.shape
    f = w_down.shape[0]
    cb = 2 * LANES
    assert l % tm == 0 and f % cb == 0
    consts = [wbr, wbs, wout, g_mix, g_pre, w_up, conv_w, conv_b, w_down, g_post]
    tile = lambda a: pl.BlockSpec((1, tm, a.shape[-1]), lambda i, j: (i, j, 0))
    acts = [x3, yr, ys, gates]
    return pl.pallas_call(
        functools.partial(_mix_ffn_body, tm, d, f, cb),
        grid=(b, l // tm),
        in_specs=[tile(a) for a in acts] + [_const_spec(c.shape) for c in consts],
        out_specs=tile(x3),
        out_shape=jax.ShapeDtypeStruct((b, l, d), F32),
        scratch_shapes=[pltpu.VMEM((SUBLANES, 2 * f), F32), pltpu.VMEM((tm, f), BF16)],
        compiler_params=_params(("arbitrary", "arbitrary")),
        name="mix_ffn",
    )(*acts, *consts)


def _block_diag(blocks):
    g, r, c = blocks.shape
    eye = jnp.eye(g, dtype=blocks.dtype)
    return (blocks[:, :, None, :] * eye[:, None, :, None]).reshape(g * r, g * c)


def _layer(x, norm_mix_pre, norm_mix_post, norm_ffn_pre, norm_ffn_post, w_in, b_gate, mu, w0, w2, a0, a2, g2,
           k_k, k_a, r_k, lnx_w, lnx_b, s5_a_re, s5_a_im, s5_b_re, s5_b_im, s5_c_re, s5_c_im, s5_d, s5_log_step,
           s5_w_glu, s5_b_glu, w_branch_rwkv, w_branch_s5, w_out, ffn_w_up, ffn_conv_w, ffn_conv_b, ffn_w_down,
           tiles):
    b, l, d = x.shape
    width = w0.shape[0]
    s5_width = s5_d.shape[0]
    n_rwkv = mu.shape[0]
    rank_w, rank_a = w2.shape[0], a2.shape[0]
    assert rank_w == HEAD and rank_a == HEAD and g2.shape[0] == LANES and b == SUBLANES
    rowv = lambda vec: vec.reshape(1, -1)
    p, u, gates = _in_proj(x, rowv(norm_mix_pre), w_in.astype(BF16), rowv(b_gate), rowv(mu), n_rwkv, s5_width,
                           tiles["in"])

    wwa = jnp.zeros((LANES, 2 * width), F32).at[:rank_w, :width].set(w2).at[rank_w:, width:].set(a2)
    ones_bd = _block_diag(jnp.ones((MXU_K // HEAD, HEAD, HEAD), F32)).astype(BF16)
    y_rwkv = _rwkv(p, rowv(w0), wwa.astype(BF16), rowv(a0), g2.astype(BF16), rowv(k_k), rowv(k_a), rowv(r_k),
                   rowv(lnx_w), rowv(lnx_b), ones_bd, width, tiles["rwkv"])

    abr, abi, bbr, bbi = _s5_disc(s5_a_re, s5_a_im, s5_log_step, s5_b_re, s5_b_im)
    n_grp = s5_a_re.shape[0]
    gh = MXU_K // S5_GROUP
    nh = n_grp // gh
    halves = lambda a: a.reshape((nh, gh) + a.shape[1:])
    bd = jax.vmap(_block_diag)
    tr = lambda a: halves(a).transpose(0, 1, 3, 2)
    b_h = jnp.concatenate([bd(tr(bbr)), bd(tr(bbi))], axis=2)
    c_h = jnp.concatenate([bd(tr(s5_c_re)), -bd(tr(s5_c_im))], axis=1)
    tt = tiles["s5"]
    ridx = jnp.arange(b * tt)
    perm = (ridx[None, :] == ((ridx % b) * tt + ridx // b)[:, None]).astype(BF16)
    y_s5 = _s5(u, perm, b_h.astype(BF16), abr.reshape(nh, 1, -1), abi.reshape(nh, 1, -1),
               c_h.astype(BF16), rowv(s5_d), s5_w_glu.astype(BF16), rowv(s5_b_glu), tt, tiles["s5_sub"])

    return _mix_ffn(x, y_rwkv, y_s5, gates, w_branch_rwkv.astype(BF16),
                    w_branch_s5.astype(BF16), w_out.astype(BF16), rowv(norm_mix_post), rowv(norm_ffn_pre),
                    ffn_w_up.astype(BF16), ffn_conv_w, rowv(ffn_conv_b), ffn_w_down.astype(BF16),
                    rowv(norm_ffn_post), tiles["ffn"])


TILES = {"in": 512, "rwkv": 512, "s5": 32, "s5_sub": 4, "ffn": 512}


def kernel(x, norm_mix_pre, norm_mix_post, norm_ffn_pre, norm_ffn_post, w_in, b_gate, rwkv_shift_mu, rwkv_w0, rwkv_w2, rwkv_a0, rwkv_a2, rwkv_g2, rwkv_k_k, rwkv_k_a, rwkv_r_k, rwkv_lnx_w, rwkv_lnx_b, s5_a_re, s5_a_im, s5_b_re, s5_b_im, s5_c_re, s5_c_im, s5_d, s5_log_step, s5_w_glu, s5_b_glu, w_branch_rwkv, w_branch_s5, w_out, ffn_w_up, ffn_conv_w, ffn_conv_b, ffn_w_down):
    depth = w_in.shape[0]
    for i in range(depth):
        x = _layer(x, norm_mix_pre[i], norm_mix_post[i], norm_ffn_pre[i], norm_ffn_post[i], w_in[i], b_gate[i],
                   rwkv_shift_mu[i], rwkv_w0[i], rwkv_w2[i], rwkv_a0[i], rwkv_a2[i], rwkv_g2[i], rwkv_k_k[i],
                   rwkv_k_a[i], rwkv_r_k[i].reshape(-1), rwkv_lnx_w[i], rwkv_lnx_b[i], s5_a_re[i], s5_a_im[i],
                   s5_b_re[i], s5_b_im[i], s5_c_re[i], s5_c_im[i], s5_d[i], s5_log_step[i], s5_w_glu[i],
                   s5_b_glu[i], w_branch_rwkv[i], w_branch_s5[i], w_out[i], ffn_w_up[i], ffn_conv_w[i],
                   ffn_conv_b[i], ffn_w_down[i], TILES)
    return x
```

```python
import functools
import math

import jax
import jax.numpy as jnp
from jax import lax
from jax.experimental import pallas as pl
from jax.experimental.pallas import tpu as pltpu

F32 = jnp.float32
BF16 = jnp.bfloat16

NORM_EPS = 1e-6
LNX_EPS = 64e-5
HEAD = 64
CHUNK = 64
WKV_GROUP = 4
FILL_PER_STAGE = 3
LANES = 128
SUBLANES = 8
S5_GROUP = 16
S5_STATE = 64
CONV_WIDTH = 3
MXU_K = 256
VMEM_LIMIT = 56 * 1024 * 1024


def _rms(x, g):
    return x * lax.rsqrt(jnp.mean(x * x, axis=-1, keepdims=True) + NORM_EPS) * g


def _split(x):
    hi = x.astype(BF16)
    lo = (x - hi.astype(F32)).astype(BF16)
    return hi, lo


_NN = (((1,), (0,)), ((), ()))
_NT = (((1,), (1,)), ((), ()))
_TN = (((0,), (0,)), ((), ()))


def _dot(a, b, dims=_NN):
    return lax.dot_general(a, b, dims, preferred_element_type=F32)


def _mm(a, b):
    return _dot(a.astype(BF16), b.astype(BF16))


def _cat0(*xs):
    return jnp.concatenate(xs, axis=0)


def _cat1(*xs):
    return jnp.concatenate(xs, axis=1)


def _const_spec(shape):
    nd = len(shape)
    return pl.BlockSpec(shape, lambda *_: (0,) * nd)


def _params(sem):
    return pltpu.CompilerParams(dimension_semantics=sem, vmem_limit_bytes=VMEM_LIMIT)


def _in_proj_body(tm, n_rwkv, n_s5, x_ref, g_ref, w_ref, bg_ref, mu_ref, p_ref, u_ref, gate_ref, carry_ref):
    @pl.when(pl.program_id(1) == 0)
    def _():
        carry_ref[...] = jnp.zeros_like(carry_ref)

    h = _rms(x_ref[0], g_ref[...]).astype(BF16)
    p = jnp.dot(h, w_ref[:, :n_rwkv], preferred_element_type=F32)
    row = lax.broadcasted_iota(jnp.int32, (tm, 1), 0)
    prev = jnp.where(row == 0, carry_ref[...], pltpu.roll(p, 1, 0))
    carry_ref[...] = p[tm - 1:tm, :]
    p_ref[0] = p + (prev - p) * mu_ref[...]
    u_ref[0] = jnp.dot(h, w_ref[:, n_rwkv:n_rwkv + n_s5], preferred_element_type=F32)
    gate_ref[0] = jax.nn.sigmoid(
        jnp.dot(h, w_ref[:, n_rwkv + n_s5:], preferred_element_type=F32) + bg_ref[...]).astype(BF16)


def _in_proj(x3, g, w_in, b_gate, mu, n_rwkv, n_s5, tm):
    b, l, d = x3.shape
    assert l % tm == 0
    n_in = w_in.shape[1]
    n_gate = n_in - n_rwkv - n_s5
    tile = lambda n: pl.BlockSpec((1, tm, n), lambda i, j: (i, j, 0))
    return pl.pallas_call(
        functools.partial(_in_proj_body, tm, n_rwkv, n_s5),
        grid=(b, l // tm),
        in_specs=[tile(d), _const_spec((1, d)), _const_spec((d, n_in)), _const_spec((1, n_gate)),
                  _const_spec((1, n_rwkv))],
        out_specs=[tile(n_rwkv), tile(n_s5), tile(n_gate)],
        out_shape=[jax.ShapeDtypeStruct((b, l, n_rwkv), F32), jax.ShapeDtypeStruct((b, l, n_s5), F32),
                   jax.ShapeDtypeStruct((b, l, n_gate), BF16)],
        scratch_shapes=[pltpu.VMEM((1, n_rwkv), F32)],
        compiler_params=_params(("arbitrary", "arbitrary")),
        name="in_proj",
    )(x3, g, w_in, b_gate, mu)


def _mstack(x, head_masks):
    z = jnp.zeros_like(x)
    return jnp.concatenate([jnp.where(m, x, z) for m in head_masks], axis=0)


def _wkv_stages(fetch, chunk_ids, hs, emit, masks):
    pair_masks, strict, incl, eye_mask = masks
    eye = eye_mask.astype(F32)
    c = CHUNK
    ms = lambda x: _mstack(x, pair_masks)
    b16 = lambda x: x.astype(BF16)
    ops = [fetch(ci) for ci in chunk_ids]
    n_pairs = len(hs)
    ids = [(k, j) for k in range(len(chunk_ids)) for j in range(n_pairs)]
    get = lambda name: [ops[k][name][j] for k, j in ids]
    rt32, at, kt, bt, bh, kh, v, pc = [get(name) for name in ("rt", "at", "kt", "bt", "bh", "kh", "v", "pc")]
    rt = [b16(x) for x in rt32]
    n = range(len(ids))
    msv = [ms(x) for x in v]
    g = [_dot(_cat0(at[i], rt[i]), _cat0(ms(bt[i]), ms(kt[i])), _NT) for i in n]
    yield
    a_ab = [jnp.where(strict, x[:c, :LANES], 0.0) for x in g]
    a_ak = [b16(jnp.where(strict, x[:c, LANES:], 0.0)) for x in g]
    a_rbk = [b16(_cat1(jnp.where(incl, x[c:, :LANES], 0.0), jnp.where(incl, x[c:, LANES:], 0.0))) for x in g]
    ab16 = [b16(x) for x in a_ab]
    q = [b16(_dot(ab16[i], ms(ab16[i]))) for i in n]
    t_inv = [eye + x for x in a_ab]
    yield
    n_sq = int(math.log2(c)) - 1
    for s in range(n_sq - 1):
        tq = [_dot(_cat0(b16(t_inv[i]), q[i]), ms(q[i])) for i in n]
        t_inv = [t_inv[i] + tq[i][:c] for i in n]
        q = [b16(tq[i][c:]) for i in n]
        yield
    t16 = [b16(t_inv[i]) for i in n]
    t_inv = [b16(t_inv[i] + _dot(t16[i], ms(q[i]))) for i in n]
    w0 = [b16(_dot(a_ak[i], msv[i])) for i in n]
    yield
    x1 = [_dot(t_inv[i], _cat1(ms(at[i]), ms(w0[i]))) for i in n]
    ap = [b16(x[:, :LANES]) for x in x1]
    u0 = [b16(x[:, LANES:]) for x in x1]
    yield
    zero = jnp.zeros((2 * c, LANES), BF16)
    ry = [_dot(a_rbk[i], _cat0(_cat1(ms(ap[i]), ms(u0[i])), _cat1(zero, msv[i]))) for i in n]
    yield
    mh = [_dot(_cat0(bh[i], kh[i]), _cat0(_cat1(ap[i], u0[i]), _cat1(zero[:c], v[i])), _TN) for i in n]
    rp = [b16(rt32[i] + ry[i][:, :LANES]) for i in n]
    yield
    blocks = lambda x: jnp.where(pair_masks[0], x[:c], x[c:])
    m = [b16(blocks(mh[i][:, :LANES]) + jnp.where(eye_mask, pc[i], 0.0)) for i in n]
    hadd = [blocks(mh[i][:, LANES:]) for i in n]

    def apply_state(k):
        for j in range(n_pairs):
            i = k * n_pairs + j
            yh = _dot(_cat0(rp[i], m[i]), ms(b16(hs[j])))
            emit(chunk_ids[k], j, yh[:c] + ry[i][:, LANES:])
            hs[j] = yh[c:] + hadd[i]

    return [functools.partial(apply_state, k) for k in range(len(chunk_ids))]


def _drive(stages, fillers, per_stage):
    while True:
        try:
            next(stages)
        except StopIteration as done:
            return done.value
        for _ in range(per_stage):
            if fillers:
                fillers.pop(0)()


def _rwkv_body(tl, width, p_ref, w0_ref, wwa_ref, a0_ref, g2_ref, kk_ref, ka_ref, rk_ref, lnw_ref, lnb_ref,
               ones_ref, y_ref, h_ref, pre_ref, twa_ref, g_ref, bonus_ref, ss_ref, rt_ref, ops_ref, pc_ref,
               yacc_ref):
    @pl.when(pl.program_id(1) == 0)
    def _():
        h_ref[...] = jnp.zeros_like(h_ref)

    c = CHUNK
    w3 = 3 * width
    n_pairs = width // LANES
    r_cols, k_cols, v_cols = slice(0, width), slice(width, 2 * width), slice(2 * width, w3)
    lane = lax.broadcasted_iota(jnp.int32, (1, LANES), 1)
    m0 = lane < HEAD
    ones = ones_ref[...]
    nk = ones.shape[0]
    segsum = lambda x: _cat1(*[_mm(x[:, i:i + nk], ones) for i in range(0, width, nk)])
    rowc = lax.broadcasted_iota(jnp.int32, (c, LANES), 0)
    colc = lax.broadcasted_iota(jnp.int32, (c, LANES), 1) % HEAD
    masks = ([m0, jnp.logical_not(m0)], colc < rowc, colc <= rowc, colc == rowc)
    ltri = (lax.broadcasted_iota(jnp.int32, (c, c), 1) <= lax.broadcasted_iota(jnp.int32, (c, c), 0)).astype(BF16)
    ltri2 = _cat1(ltri, ltri)

    def low_rank(rows):
        wa = p_ref[0, rows, w3:w3 + LANES]
        twa_ref[rows, :] = _mm(jnp.where(m0, jnp.tanh(wa), wa), wwa_ref[...])
        g_ref[rows, :] = _mm(jax.nn.sigmoid(p_ref[0, rows, w3 + LANES:]), g2_ref[...])

    def elementwise(rows):
        pre_ref[3, rows, :] = -math.exp(-0.5) * jax.nn.sigmoid(w0_ref[...] + twa_ref[rows, :width])
        a = jax.nn.sigmoid(a0_ref[...] + twa_ref[rows, width:])
        k = p_ref[0, rows, k_cols]
        k2 = k * (1.0 + (a - 1.0) * ka_ref[...])
        pre_ref[0, rows, :] = k2
        pre_ref[1, rows, :] = k * kk_ref[...]
        pre_ref[2, rows, :] = a
        bonus_ref[rows, :] = p_ref[0, rows, r_cols] * k2 * rk_ref[...]

    def head_sums(rows):
        kk = pre_ref[1, rows, :]
        ss_ref[rows, :] = segsum(kk * kk)
        bonus_ref[rows, :] = segsum(bonus_ref[rows, :])

    def normalise(rows):
        kk = pre_ref[1, rows, :] * lax.rsqrt(jnp.maximum(ss_ref[rows, :], 1e-24))
        pre_ref[2, rows, :] = kk * pre_ref[2, rows, :]
        pre_ref[1, rows, :] = -kk
        bonus_ref[rows, :] = bonus_ref[rows, :] * p_ref[0, rows, v_cols]

    def decay_scale(ci):
        rows = slice(ci * c, (ci + 1) * c)
        lwc = pre_ref[3, rows, :]
        cum = _dot(ltri2, _cat0(*_split(lwc)))
        pw = jnp.exp(cum)
        pinv = jnp.exp(-cum)
        kt = pre_ref[0, rows, :] * pinv
        bt = pre_ref[2, rows, :] * pinv
        pc = pw[c - 1:c, :]
        rt_ref[rows, :] = p_ref[0, rows, r_cols] * pw
        for i, x in enumerate((pre_ref[1, rows, :] * jnp.exp(cum - lwc), kt, bt, bt * pc, kt * pc,
                               p_ref[0, rows, v_cols])):
            ops_ref[i, rows, :] = x.astype(BF16)
        pc_ref[ci:ci + 1, :] = pc

    def fetch(ci):
        rows = slice(ci * c, (ci + 1) * c)
        tiles = lambda x: [x[:, j * LANES:(j + 1) * LANES] for j in range(n_pairs)]
        out = {name: tiles(ops_ref[i, rows, :]) for i, name in enumerate(("at", "kt", "bt", "bh", "kh", "v"))}
        out["rt"] = tiles(rt_ref[rows, :])
        out["pc"] = tiles(pc_ref[ci:ci + 1, :])
        return out

    def centre(rows):
        y = yacc_ref[rows, :]
        yacc_ref[rows, :] = y - segsum(y) * (1.0 / HEAD)

    def group_norm(rows):
        yc = yacc_ref[rows, :]
        var = segsum(yc * yc) * (1.0 / HEAD)
        yn = yc * lax.rsqrt(var + LNX_EPS) * lnw_ref[...] + lnb_ref[...]
        y_ref[0, rows, :] = ((yn + bonus_ref[rows, :]) * g_ref[rows, :]).astype(BF16)

    gc = WKV_GROUP
    n_groups = tl // (c * gc)
    half = c * gc // 2

    def prepare(gi):
        r0 = gi * gc * c
        blocks = [slice(r0 + i * c, r0 + (i + 1) * c) for i in range(gc)]
        halves = [slice(r0, r0 + half), slice(r0 + half, r0 + 2 * half)]
        return ([functools.partial(low_rank, h) for h in halves] + [functools.partial(elementwise, b) for b in blocks]
                + [functools.partial(head_sums, h) for h in halves] + [functools.partial(normalise, b) for b in blocks]
                + [functools.partial(decay_scale, ci) for ci in range(gi * gc, (gi + 1) * gc)])

    def finish(gi):
        r0 = gi * gc * c
        halves = [slice(r0, r0 + half), slice(r0 + half, r0 + 2 * half)]
        return [functools.partial(f, h) for f in (centre, group_norm) for h in halves]

    def emit(ci, j, y):
        yacc_ref[ci * c:(ci + 1) * c, j * LANES:(j + 1) * LANES] = y

    for thunk in prepare(0):
        thunk()
    hs = [h_ref[j] for j in range(n_pairs)]
    state_steps = []
    for gi in range(n_groups):
        prep = prepare(gi + 1) if gi + 1 < n_groups else []
        fillers = []
        for step in state_steps:
            fillers += [step] + prep[:2]
            prep = prep[2:]
        fillers += prep + (finish(gi - 1) if gi > 0 else [])
        chunk_ids = list(range(gi * gc, (gi + 1) * gc))
        state_steps = _drive(_wkv_stages(fetch, chunk_ids, hs, emit, masks), fillers, FILL_PER_STAGE)
        for thunk in fillers:
            thunk()
    for thunk in state_steps + finish(n_groups - 1):
        thunk()
    for j in range(n_pairs):
        h_ref[j] = hs[j]


def _rwkv(p3, w0, wwa, a0, g2, k_k, k_a, r_k, lnx_w, lnx_b, ones_bd, width, tl):
    b, l, n_rwkv = p3.shape
    assert l % tl == 0 and tl % (CHUNK * WKV_GROUP) == 0 and (tl // CHUNK) % SUBLANES == 0
    n_pairs = width // LANES
    consts = [w0, wwa, a0, g2, k_k, k_a, r_k, lnx_w, lnx_b, ones_bd]
    rows = lambda n=width: pltpu.VMEM((tl, n), F32)
    return pl.pallas_call(
        functools.partial(_rwkv_body, tl, width),
        grid=(b, l // tl),
        in_specs=[pl.BlockSpec((1, tl, n_rwkv), lambda i, j: (i, j, 0))] + [_const_spec(c.shape) for c in consts],
        out_specs=pl.BlockSpec((1, tl, width), lambda i, j: (i, j, 0)),
        out_shape=jax.ShapeDtypeStruct((b, l, width), BF16),
        scratch_shapes=[pltpu.VMEM((n_pairs, HEAD, LANES), F32), pltpu.VMEM((4, tl, width), F32),
                        rows(2 * width), rows(), rows(), rows(), rows(), pltpu.VMEM((6, tl, width), BF16),
                        pltpu.VMEM((tl // CHUNK, width), F32), rows()],
        compiler_params=_params(("arbitrary", "arbitrary")),
        name="rwkv7",
    )(p3, *consts)


def _s5_disc_body(are_ref, aim_ref, dt_ref, bre_ref, bim_ref, abr_ref, abi_ref, bbr_ref, bbi_ref):
    lre, lim, dt = are_ref[...], aim_ref[...], jnp.exp(dt_ref[...])
    mag = jnp.exp(lre * dt)
    abr = mag * jnp.cos(lim * dt)
    abi = mag * jnp.sin(lim * dt)
    inv = 1.0 / (lre * lre + lim * lim)
    nr, ni = abr - 1.0, abi
    cr = (nr * lre + ni * lim) * inv
    ci = (ni * lre - nr * lim) * inv
    bre, bim = bre_ref[...], bim_ref[...]
    abr_ref[...] = abr
    abi_ref[...] = abi
    bbr_ref[...] = cr * bre - ci * bim
    bbi_ref[...] = cr * bim + ci * bre


def _s5_disc(a_re, a_im, log_step, b_re, b_im):
    g, p, c = b_re.shape
    col = lambda x: x.reshape(g * p, 1)
    dt = jnp.broadcast_to(log_step[:, None], (g, p))
    outs = pl.pallas_call(
        _s5_disc_body,
        out_shape=[jax.ShapeDtypeStruct((g * p, 1), F32)] * 2 + [jax.ShapeDtypeStruct((g * p, c), F32)] * 2,
        name="s5_disc",
    )(col(a_re), col(a_im), col(dt), b_re.reshape(g * p, c), b_im.reshape(g * p, c))
    abr, abi, bbr, bbi = outs
    return abr.reshape(g, p), abi.reshape(g, p), bbr.reshape(g, p, c), bbi.reshape(g, p, c)


def _s5_body(tt, n_sub, nh, col_w, u_ref, perm_ref, permt_ref, b_ref, are_ref, aim_ref, c_ref, d_ref, wglu_ref,
             bglu_ref, y_ref, st_ref, s_ref):
    @pl.when(pl.program_id(0) == 0)
    def _():
        s_ref[...] = jnp.zeros_like(s_ref)

    nb = SUBLANES
    rows = nb * tt
    width = u_ref.shape[-1]
    kin = width // nh
    nsh = are_ref.shape[-1]
    permt = permt_ref[...]
    permt2 = _cat1(permt, permt)
    us = [u_ref[:, i * tt:(i + 1) * tt, :].reshape(rows, width) for i in range(n_sub)]
    for i in range(n_sub):
        u_tm = jnp.dot(perm_ref[...], us[i].astype(BF16), preferred_element_type=F32).astype(BF16)
        for h in range(nh):
            st_ref[i, h] = jnp.dot(u_tm[:, h * kin:(h + 1) * kin], b_ref[h], preferred_element_type=F32)
    for i in range(n_sub):
        for h in range(nh):
            for cg in range(nsh // col_w):
                re_c = slice(cg * col_w, (cg + 1) * col_w)
                im_c = slice(nsh + cg * col_w, nsh + (cg + 1) * col_w)
                are = jnp.broadcast_to(are_ref[h, :, re_c], (nb, col_w))
                aim = jnp.broadcast_to(aim_ref[h, :, re_c], (nb, col_w))

                def step(t, carry):
                    sre, sim = carry
                    r8 = pl.ds(pl.multiple_of(t * nb, nb), nb)
                    nre = are * sre - aim * sim + st_ref[i, h, r8, re_c]
                    nim = are * sim + aim * sre + st_ref[i, h, r8, im_c]
                    st_ref[i, h, r8, re_c] = nre
                    st_ref[i, h, r8, im_c] = nim
                    return nre, nim

                sre, sim = lax.fori_loop(0, tt, step, (s_ref[h, :, re_c], s_ref[h, :, im_c]), unroll=True)
                s_ref[h, :, re_c] = sre
                s_ref[h, :, im_c] = sim
        y_tm = jnp.concatenate([jnp.dot(st_ref[i, h].astype(BF16), c_ref[h], preferred_element_type=F32)
                                for h in range(nh)], axis=1)
        y = _dot(permt2, _cat0(*_split(y_tm)))
        y = jax.nn.gelu(y + d_ref[...] * us[i])
        y = y * jax.nn.sigmoid(jnp.dot(y.astype(BF16), wglu_ref[...], preferred_element_type=F32) + bglu_ref[...])
        y_ref[:, i * tt:(i + 1) * tt, :] = y.astype(BF16).reshape(nb, tt, width)


def _s5(u3, perm, b_h, are, aim, c_h, d, w_glu, b_glu, tt, n_sub):
    b, l, width = u3.shape
    assert l % (tt * n_sub) == 0 and b == SUBLANES
    nh, _, nsh2 = b_h.shape
    consts = [perm, perm.T, b_h, are, aim, c_h, d, w_glu, b_glu]
    blk = pl.BlockSpec((b, tt * n_sub, width), lambda i: (0, i, 0))
    return pl.pallas_call(
        functools.partial(_s5_body, tt, n_sub, nh, 4 * LANES),
        grid=(l // (tt * n_sub),),
        in_specs=[blk] + [_const_spec(c.shape) for c in consts],
        out_specs=blk,
        out_shape=jax.ShapeDtypeStruct((b, l, width), BF16),
        scratch_shapes=[pltpu.VMEM((n_sub, nh, b * tt, nsh2), F32), pltpu.VMEM((nh, b, nsh2), F32)],
        compiler_params=_params(("arbitrary",)),
        name="s5",
    )(u3, *consts)


def _mix_ffn_body(tm, d, f, cb, x_ref, yr_ref, ys_ref, gate_ref, wbr_ref, wbs_ref, wout_ref, gmix_ref,
                  gpre_ref, wup_ref, cw_ref, cbias_ref, wdown_ref, gpost_ref, o_ref, carry_ref, act_ref):
    @pl.when(pl.program_id(1) == 0)
    def _():
        carry_ref[...] = jnp.zeros_like(carry_ref)

    o_r = jnp.dot(yr_ref[0], wbr_ref[...], preferred_element_type=F32)
    o_s = jnp.dot(ys_ref[0], wbs_ref[...], preferred_element_type=F32)
    mixed = gate_ref[0, :, :d] * o_r + gate_ref[0, :, d:] * o_s
    mixed = jnp.dot(mixed.astype(BF16), wout_ref[...], preferred_element_type=F32)
    x = x_ref[0] + _rms(mixed, gmix_ref[...])

    h = _rms(x, gpre_ref[...]).astype(BF16)
    nb = SUBLANES
    sub = lax.broadcasted_iota(jnp.int32, (nb, 1), 0)

    def conv(cols):
        z = jnp.dot(h, wup_ref[:, cols], preferred_element_type=F32)
        tail = carry_ref[:, cols]
        carry_ref[:, cols] = z[tm - nb:, :]
        out = cbias_ref[:, cols] + cw_ref[CONV_WIDTH - 1:CONV_WIDTH, cols] * z
        for s in range(1, CONV_WIDTH):
            zr = pltpu.roll(z, s, 0)
            top = jnp.where(sub < s, pltpu.roll(tail, s, 0), zr[:nb])
            zs = jnp.concatenate([top, zr[nb:]], axis=0)
            out = out + cw_ref[CONV_WIDTH - 1 - s:CONV_WIDTH - s, cols] * zs
        return out

    for j in range(f // cb):
        gate = conv(slice(j * cb, (j + 1) * cb))
        val = conv(slice(f + j * cb, f + (j + 1) * cb))
        act_ref[:, j * cb:(j + 1) * cb] = (jax.nn.gelu(gate) * val).astype(BF16)
    y = jnp.dot(act_ref[...], wdown_ref[...], preferred_element_type=F32)
    o_ref[0] = x + _rms(y, gpost_ref[...])


def _mix_ffn(x3, yr, ys, gates, wbr, wbs, wout, g_mix, g_pre, w_up, conv_w, conv_b, w_down, g_post, tm):
    b, l, d = x3.shape
    f = w_down.shape[0]
    cb = 2 * LANES
    assert l % tm == 0 and f % cb == 0
    consts = [wbr, wbs, wout, g_mix, g_pre, w_up, conv_w, conv_b, w_down, g_post]
    tile = lambda a: pl.BlockSpec((1, tm, a.shape[-1]), lambda i, j: (i, j, 0))
    acts = [x3, yr, ys, gates]
    return pl.pallas_call(
        functools.partial(_mix_ffn_body, tm, d, f, cb),
        grid=(b, l // tm),
        in_specs=[tile(a) for a in acts] + [_const_spec(c.shape) for c in consts],
        out_specs=tile(x3),
        out_shape=jax.ShapeDtypeStruct((b, l, d), F32),
        scratch_shapes=[pltpu.VMEM((SUBLANES, 2 * f), F32), pltpu.VMEM((tm, f), BF16)],
        compiler_params=_params(("arbitrary", "arbitrary")),
        name="mix_ffn",
    )(*acts, *consts)


def _block_diag(blocks):
    g, r, c = blocks.shape
    eye = jnp.eye(g, dtype=blocks.dtype)
    return (blocks[:, :, None, :] * eye[:, None, :, None]).reshape(g * r, g * c)


def _layer(x, norm_mix_pre, norm_mix_post, norm_ffn_pre, norm_ffn_post, w_in, b_gate, mu, w0, w2, a0, a2, g2,
           k_k, k_a, r_k, lnx_w, lnx_b, s5_a_re, s5_a_im, s5_b_re, s5_b_im, s5_c_re, s5_c_im, s5_d, s5_log_step,
           s5_w_glu, s5_b_glu, w_branch_rwkv, w_branch_s5, w_out, ffn_w_up, ffn_conv_w, ffn_conv_b, ffn_w_down,
           tiles):
    b, l, d = x.shape
    width = w0.shape[0]
    s5_width = s5_d.shape[0]
    n_rwkv = mu.shape[0]
    rank_w, rank_a = w2.shape[0], a2.shape[0]
    assert rank_w == HEAD and rank_a == HEAD and g2.shape[0] == LANES and b == SUBLANES
    rowv = lambda vec: vec.reshape(1, -1)
    p, u, gates = _in_proj(x, rowv(norm_mix_pre), w_in.astype(BF16), rowv(b_gate), rowv(mu), n_rwkv, s5_width,
                           tiles["in"])

    wwa = jnp.zeros((LANES, 2 * width), F32).at[:rank_w, :width].set(w2).at[rank_w:, width:].set(a2)
    ones_bd = _block_diag(jnp.ones((MXU_K // HEAD, HEAD, HEAD), F32)).astype(BF16)
    y_rwkv = _rwkv(p, rowv(w0), wwa.astype(BF16), rowv(a0), g2.astype(BF16), rowv(k_k), rowv(k_a), rowv(r_k),
                   rowv(lnx_w), rowv(lnx_b), ones_bd, width, tiles["rwkv"])

    abr, abi, bbr, bbi = _s5_disc(s5_a_re, s5_a_im, s5_log_step, s5_b_re, s5_b_im)
    n_grp = s5_a_re.shape[0]
    gh = LANES // S5_GROUP
    nh = n_grp // gh
    halves = lambda a: a.reshape((nh, gh) + a.shape[1:])
    bd = jax.vmap(_block_diag)
    tr = lambda a: halves(a).transpose(0, 1, 3, 2)
    b_h = jnp.concatenate([bd(tr(bbr)), bd(tr(bbi))], axis=2)
    c_h = jnp.concatenate([bd(tr(s5_c_re)), -bd(tr(s5_c_im))], axis=1)
    tt = tiles["s5"]
    ridx = jnp.arange(b * tt)
    perm = (ridx[None, :] == ((ridx % b) * tt + ridx // b)[:, None]).astype(BF16)
    y_s5 = _s5(u, perm, b_h.astype(BF16), abr.reshape(nh, 1, -1), abi.reshape(nh, 1, -1),
               c_h.astype(BF16), rowv(s5_d), s5_w_glu.astype(BF16), rowv(s5_b_glu), tt, tiles["s5_sub"])

    return _mix_ffn(x, y_rwkv, y_s5, gates, w_branch_rwkv.astype(BF16),
                    w_branch_s5.astype(BF16), w_out.astype(BF16), rowv(norm_mix_post), rowv(norm_ffn_pre),
                    ffn_w_up.astype(BF16), ffn_conv_w, rowv(ffn_conv_b), ffn_w_down.astype(BF16),
                    rowv(norm_ffn_post), tiles["ffn"])


TILES = {"in": 512, "rwkv": 1024, "s5": 32, "s5_sub": 4, "ffn": 512}


def kernel(x, norm_mix_pre, norm_mix_post, norm_ffn_pre, norm_ffn_post, w_in, b_gate, rwkv_shift_mu, rwkv_w0, rwkv_w2, rwkv_a0, rwkv_a2, rwkv_g2, rwkv_k_k, rwkv_k_a, rwkv_r_k, rwkv_lnx_w, rwkv_lnx_b, s5_a_re, s5_a_im, s5_b_re, s5_b_im, s5_c_re, s5_c_im, s5_d, s5_log_step, s5_w_glu, s5_b_glu, w_branch_rwkv, w_branch_s5, w_out, ffn_w_up, ffn_conv_w, ffn_conv_b, ffn_w_down):
    depth = w_in.shape[0]
    for i in range(depth):
        x = _layer(x, norm_mix_pre[i], norm_mix_post[i], norm_ffn_pre[i], norm_ffn_post[i], w_in[i], b_gate[i],
                   rwkv_shift_mu[i], rwkv_w0[i], rwkv_w2[i], rwkv_a0[i], rwkv_a2[i], rwkv_g2[i], rwkv_k_k[i],
                   rwkv_k_a[i], rwkv_r_k[i].reshape(-1), rwkv_lnx_w[i], rwkv_lnx_b[i], s5_a_re[i], s5_a_im[i],
                   s5_b_re[i], s5_b_im[i], s5_c_re[i], s5_c_im[i], s5_d[i], s5_log_step[i], s5_w_glu[i],
                   s5_b_glu[i], w_branch_rwkv[i], w_branch_s5[i], w_out[i], ffn_w_up[i], ffn_conv_w[i],
                   ffn_conv_b[i], ffn_w_down[i], TILES)
    return x
```

```python
import functools
import math

import jax
import jax.numpy as jnp
from jax import lax
from jax.experimental import pallas as pl
from jax.experimental.pallas import tpu as pltpu

F32 = jnp.float32
BF16 = jnp.bfloat16

NORM_EPS = 1e-6
LNX_EPS = 64e-5
HEAD = 64
CHUNK = 64
WKV_GROUP = 4
FILL_PER_STAGE = 3
LANES = 128
SUBLANES = 8
S5_GROUP = 16
S5_STATE = 64
CONV_WIDTH = 3
MXU_K = 256
VMEM_LIMIT = 56 * 1024 * 1024


def _rms(x, g):
    return x * lax.rsqrt(jnp.mean(x * x, axis=-1, keepdims=True) + NORM_EPS) * g


def _split(x):
    hi = x.astype(BF16)
    lo = (x - hi.astype(F32)).astype(BF16)
    return hi, lo


_NN = (((1,), (0,)), ((), ()))
_NT = (((1,), (1,)), ((), ()))
_TN = (((0,), (0,)), ((), ()))


def _dot(a, b, dims=_NN):
    return lax.dot_general(a, b, dims, preferred_element_type=F32)


def _mm(a, b):
    return _dot(a.astype(BF16), b.astype(BF16))


def _cat0(*xs):
    return jnp.concatenate(xs, axis=0)


def _cat1(*xs):
    return jnp.concatenate(xs, axis=1)


def _const_spec(shape):
    nd = len(shape)
    return pl.BlockSpec(shape, lambda *_: (0,) * nd)


def _params(sem):
    return pltpu.CompilerParams(dimension_semantics=sem, vmem_limit_bytes=VMEM_LIMIT)


def _in_proj_body(tm, n_rwkv, n_s5, x_ref, g_ref, w32_ref, bg_ref, mu_ref, p_ref, u_ref, gate_ref, carry_ref,
                  w_ref):
    @pl.when((pl.program_id(0) == 0) & (pl.program_id(1) == 0))
    def _():
        for c0 in range(0, w_ref.shape[1], MXU_K):
            w_ref[:, c0:c0 + MXU_K] = w32_ref[:, c0:c0 + MXU_K].astype(BF16)

    @pl.when(pl.program_id(1) == 0)
    def _():
        carry_ref[...] = jnp.zeros_like(carry_ref)

    h = _rms(x_ref[0], g_ref[...]).astype(BF16)
    p = jnp.dot(h, w_ref[:, :n_rwkv], preferred_element_type=F32)
    row = lax.broadcasted_iota(jnp.int32, (tm, 1), 0)
    prev = jnp.where(row == 0, carry_ref[...], pltpu.roll(p, 1, 0))
    carry_ref[...] = p[tm - 1:tm, :]
    p_ref[0] = p + (prev - p) * mu_ref[...]
    u_ref[0] = jnp.dot(h, w_ref[:, n_rwkv:n_rwkv + n_s5], preferred_element_type=F32)
    gate_ref[0] = jax.nn.sigmoid(
        jnp.dot(h, w_ref[:, n_rwkv + n_s5:], preferred_element_type=F32) + bg_ref[...]).astype(BF16)


def _in_proj(x3, g, w_in, b_gate, mu, n_rwkv, n_s5, tm):
    b, l, d = x3.shape
    assert l % tm == 0
    n_in = w_in.shape[1]
    n_gate = n_in - n_rwkv - n_s5
    tile = lambda n: pl.BlockSpec((1, tm, n), lambda i, j: (i, j, 0))
    return pl.pallas_call(
        functools.partial(_in_proj_body, tm, n_rwkv, n_s5),
        grid=(b, l // tm),
        in_specs=[tile(d), _const_spec((1, d)), _const_spec((d, n_in)), _const_spec((1, n_gate)),
                  _const_spec((1, n_rwkv))],
        out_specs=[tile(n_rwkv), tile(n_s5), tile(n_gate)],
        out_shape=[jax.ShapeDtypeStruct((b, l, n_rwkv), F32), jax.ShapeDtypeStruct((b, l, n_s5), F32),
                   jax.ShapeDtypeStruct((b, l, n_gate), BF16)],
        scratch_shapes=[pltpu.VMEM((1, n_rwkv), F32), pltpu.VMEM((d, n_in), BF16)],
        compiler_params=_params(("arbitrary", "arbitrary")),
        name="in_proj",
    )(x3, g, w_in, b_gate, mu)


def _mstack(x, head_masks):
    z = jnp.zeros_like(x)
    return jnp.concatenate([jnp.where(m, x, z) for m in head_masks], axis=0)


def _wkv_stages(fetch, chunk_ids, hs, emit, masks):
    pair_masks, strict, incl, eye_mask = masks
    eye = eye_mask.astype(F32)
    c = CHUNK
    ms = lambda x: _mstack(x, pair_masks)
    b16 = lambda x: x.astype(BF16)
    ops = [fetch(ci) for ci in chunk_ids]
    n_pairs = len(hs)
    ids = [(k, j) for k in range(len(chunk_ids)) for j in range(n_pairs)]
    get = lambda name: [ops[k][name][j] for k, j in ids]
    rt32, at, kt, bt, bh, kh, v, pc = [get(name) for name in ("rt", "at", "kt", "bt", "bh", "kh", "v", "pc")]
    rt = [b16(x) for x in rt32]
    n = range(len(ids))
    msv = [ms(x) for x in v]
    g = [_dot(_cat0(at[i], rt[i]), _cat0(ms(bt[i]), ms(kt[i])), _NT) for i in n]
    yield
    a_ab = [jnp.where(strict, x[:c, :LANES], 0.0) for x in g]
    a_ak = [b16(jnp.where(strict, x[:c, LANES:], 0.0)) for x in g]
    a_rbk = [b16(_cat1(jnp.where(incl, x[c:, :LANES], 0.0), jnp.where(incl, x[c:, LANES:], 0.0))) for x in g]
    ab16 = [b16(x) for x in a_ab]
    q = [b16(_dot(ab16[i], ms(ab16[i]))) for i in n]
    t_inv = [eye + x for x in a_ab]
    yield
    n_sq = int(math.log2(c)) - 1
    for s in range(n_sq - 1):
        tq = [_dot(_cat0(b16(t_inv[i]), q[i]), ms(q[i])) for i in n]
        t_inv = [t_inv[i] + tq[i][:c] for i in n]
        q = [b16(tq[i][c:]) for i in n]
        yield
    t16 = [b16(t_inv[i]) for i in n]
    t_inv = [b16(t_inv[i] + _dot(t16[i], ms(q[i]))) for i in n]
    w0 = [b16(_dot(a_ak[i], msv[i])) for i in n]
    yield
    x1 = [_dot(t_inv[i], _cat1(ms(at[i]), ms(w0[i]))) for i in n]
    ap = [b16(x[:, :LANES]) for x in x1]
    u0 = [b16(x[:, LANES:]) for x in x1]
    yield
    zero = jnp.zeros((2 * c, LANES), BF16)
    ry = [_dot(a_rbk[i], _cat0(_cat1(ms(ap[i]), ms(u0[i])), _cat1(zero, msv[i]))) for i in n]
    yield
    mh = [_dot(_cat0(bh[i], kh[i]), _cat0(_cat1(ap[i], u0[i]), _cat1(zero[:c], v[i])), _TN) for i in n]
    rp = [b16(rt32[i] + ry[i][:, :LANES]) for i in n]
    yield
    blocks = lambda x: jnp.where(pair_masks[0], x[:c], x[c:])
    m = [b16(blocks(mh[i][:, :LANES]) + jnp.where(eye_mask, pc[i], 0.0)) for i in n]
    hadd = [blocks(mh[i][:, LANES:]) for i in n]

    def apply_state(k):
        for j in range(n_pairs):
            i = k * n_pairs + j
            yh = _dot(_cat0(rp[i], m[i]), ms(b16(hs[j])))
            emit(chunk_ids[k], j, yh[:c] + ry[i][:, LANES:])
            hs[j] = yh[c:] + hadd[i]

    return [functools.partial(apply_state, k) for k in range(len(chunk_ids))]


def _drive(stages, fillers, per_stage):
    while True:
        try:
            next(stages)
        except StopIteration as done:
            return done.value
        for _ in range(per_stage):
            if fillers:
                fillers.pop(0)()


def _rwkv_body(tl, width, p_ref, w0_ref, wwa_ref, a0_ref, g2_ref, kk_ref, ka_ref, rk_ref, lnw_ref, lnb_ref,
               ones_ref, y_ref, h_ref, pre_ref, twa_ref, g_ref, bonus_ref, ss_ref, rt_ref, ops_ref, pc_ref,
               yacc_ref):
    @pl.when(pl.program_id(1) == 0)
    def _():
        h_ref[...] = jnp.zeros_like(h_ref)

    c = CHUNK
    w3 = 3 * width
    n_pairs = width // LANES
    r_cols, k_cols, v_cols = slice(0, width), slice(width, 2 * width), slice(2 * width, w3)
    lane = lax.broadcasted_iota(jnp.int32, (1, LANES), 1)
    m0 = lane < HEAD
    ones = ones_ref[...]
    nk = ones.shape[0]
    segsum = lambda x: _cat1(*[_mm(x[:, i:i + nk], ones) for i in range(0, width, nk)])
    rowc = lax.broadcasted_iota(jnp.int32, (c, LANES), 0)
    colc = lax.broadcasted_iota(jnp.int32, (c, LANES), 1) % HEAD
    masks = ([m0, jnp.logical_not(m0)], colc < rowc, colc <= rowc, colc == rowc)
    ltri = (lax.broadcasted_iota(jnp.int32, (c, c), 1) <= lax.broadcasted_iota(jnp.int32, (c, c), 0)).astype(BF16)
    ltri2 = _cat1(ltri, ltri)

    def low_rank(rows):
        wa = p_ref[0, rows, w3:w3 + LANES]
        twa_ref[rows, :] = _mm(jnp.where(m0, jnp.tanh(wa), wa), wwa_ref[...])
        g_ref[rows, :] = _mm(jax.nn.sigmoid(p_ref[0, rows, w3 + LANES:]), g2_ref[...])

    def elementwise(rows):
        pre_ref[3, rows, :] = -math.exp(-0.5) * jax.nn.sigmoid(w0_ref[...] + twa_ref[rows, :width])
        a = jax.nn.sigmoid(a0_ref[...] + twa_ref[rows, width:])
        k = p_ref[0, rows, k_cols]
        k2 = k * (1.0 + (a - 1.0) * ka_ref[...])
        pre_ref[0, rows, :] = k2
        pre_ref[1, rows, :] = k * kk_ref[...]
        pre_ref[2, rows, :] = a
        bonus_ref[rows, :] = p_ref[0, rows, r_cols] * k2 * rk_ref[...]

    def head_sums(rows):
        kk = pre_ref[1, rows, :]
        ss_ref[rows, :] = segsum(kk * kk)
        bonus_ref[rows, :] = segsum(bonus_ref[rows, :])

    def normalise(rows):
        kk = pre_ref[1, rows, :] * lax.rsqrt(jnp.maximum(ss_ref[rows, :], 1e-24))
        pre_ref[2, rows, :] = kk * pre_ref[2, rows, :]
        pre_ref[1, rows, :] = -kk
        bonus_ref[rows, :] = bonus_ref[rows, :] * p_ref[0, rows, v_cols]

    def decay_scale(ci):
        rows = slice(ci * c, (ci + 1) * c)
        lwc = pre_ref[3, rows, :]
        cum = _dot(ltri2, _cat0(*_split(lwc)))
        pw = jnp.exp(cum)
        pinv = jnp.exp(-cum)
        kt = pre_ref[0, rows, :] * pinv
        bt = pre_ref[2, rows, :] * pinv
        pc = pw[c - 1:c, :]
        rt_ref[rows, :] = p_ref[0, rows, r_cols] * pw
        for i, x in enumerate((pre_ref[1, rows, :] * jnp.exp(cum - lwc), kt, bt, bt * pc, kt * pc,
                               p_ref[0, rows, v_cols])):
            ops_ref[i, rows, :] = x.astype(BF16)
        pc_ref[ci:ci + 1, :] = pc

    def fetch(ci):
        rows = slice(ci * c, (ci + 1) * c)
        tiles = lambda x: [x[:, j * LANES:(j + 1) * LANES] for j in range(n_pairs)]
        out = {name: tiles(ops_ref[i, rows, :]) for i, name in enumerate(("at", "kt", "bt", "bh", "kh", "v"))}
        out["rt"] = tiles(rt_ref[rows, :])
        out["pc"] = tiles(pc_ref[ci:ci + 1, :])
        return out

    def centre(rows):
        y = yacc_ref[rows, :]
        yacc_ref[rows, :] = y - segsum(y) * (1.0 / HEAD)

    def group_norm(rows):
        yc = yacc_ref[rows, :]
        var = segsum(yc * yc) * (1.0 / HEAD)
        yn = yc * lax.rsqrt(var + LNX_EPS) * lnw_ref[...] + lnb_ref[...]
        y_ref[0, rows, :] = ((yn + bonus_ref[rows, :]) * g_ref[rows, :]).astype(BF16)

    gc = WKV_GROUP
    n_groups = tl // (c * gc)
    half = c * gc // 2

    def prepare(gi):
        r0 = gi * gc * c
        blocks = [slice(r0 + i * c, r0 + (i + 1) * c) for i in range(gc)]
        halves = [slice(r0, r0 + half), slice(r0 + half, r0 + 2 * half)]
        return ([functools.partial(low_rank, h) for h in halves] + [functools.partial(elementwise, b) for b in blocks]
                + [functools.partial(head_sums, h) for h in halves] + [functools.partial(normalise, b) for b in blocks]
                + [functools.partial(decay_scale, ci) for ci in range(gi * gc, (gi + 1) * gc)])

    def finish(gi):
        r0 = gi * gc * c
        halves = [slice(r0, r0 + half), slice(r0 + half, r0 + 2 * half)]
        return [functools.partial(f, h) for f in (centre, group_norm) for h in halves]

    def emit(ci, j, y):
        yacc_ref[ci * c:(ci + 1) * c, j * LANES:(j + 1) * LANES] = y

    for thunk in prepare(0):
        thunk()
    hs = [h_ref[j] for j in range(n_pairs)]
    state_steps = []
    for gi in range(n_groups):
        prep = prepare(gi + 1) if gi + 1 < n_groups else []
        fillers = []
        for step in state_steps:
            fillers += [step] + prep[:2]
            prep = prep[2:]
        fillers += prep + (finish(gi - 1) if gi > 0 else [])
        chunk_ids = list(range(gi * gc, (gi + 1) * gc))
        state_steps = _drive(_wkv_stages(fetch, chunk_ids, hs, emit, masks), fillers, FILL_PER_STAGE)
        for thunk in fillers:
            thunk()
    for thunk in state_steps + finish(n_groups - 1):
        thunk()
    for j in range(n_pairs):
        h_ref[j] = hs[j]


def _rwkv(p3, w0, wwa, a0, g2, k_k, k_a, r_k, lnx_w, lnx_b, ones_bd, width, tl):
    b, l, n_rwkv = p3.shape
    assert l % tl == 0 and tl % (CHUNK * WKV_GROUP) == 0 and (tl // CHUNK) % SUBLANES == 0
    n_pairs = width // LANES
    consts = [w0, wwa, a0, g2, k_k, k_a, r_k, lnx_w, lnx_b, ones_bd]
    rows = lambda n=width: pltpu.VMEM((tl, n), F32)
    return pl.pallas_call(
        functools.partial(_rwkv_body, tl, width),
        grid=(b, l // tl),
        in_specs=[pl.BlockSpec((1, tl, n_rwkv), lambda i, j: (i, j, 0))] + [_const_spec(c.shape) for c in consts],
        out_specs=pl.BlockSpec((1, tl, width), lambda i, j: (i, j, 0)),
        out_shape=jax.ShapeDtypeStruct((b, l, width), BF16),
        scratch_shapes=[pltpu.VMEM((n_pairs, HEAD, LANES), F32), pltpu.VMEM((4, tl, width), F32),
                        rows(2 * width), rows(), rows(), rows(), rows(), pltpu.VMEM((6, tl, width), BF16),
                        pltpu.VMEM((tl // CHUNK, width), F32), rows()],
        compiler_params=_params(("arbitrary", "arbitrary")),
        name="rwkv7",
    )(p3, *consts)


def _s5_disc_body(are_ref, aim_ref, dt_ref, bre_ref, bim_ref, abr_ref, abi_ref, bbr_ref, bbi_ref):
    lre, lim, dt = are_ref[...], aim_ref[...], jnp.exp(dt_ref[...])
    mag = jnp.exp(lre * dt)
    abr = mag * jnp.cos(lim * dt)
    abi = mag * jnp.sin(lim * dt)
    inv = 1.0 / (lre * lre + lim * lim)
    nr, ni = abr - 1.0, abi
    cr = (nr * lre + ni * lim) * inv
    ci = (ni * lre - nr * lim) * inv
    bre, bim = bre_ref[...], bim_ref[...]
    abr_ref[...] = abr
    abi_ref[...] = abi
    bbr_ref[...] = cr * bre - ci * bim
    bbi_ref[...] = cr * bim + ci * bre


def _s5_disc(a_re, a_im, log_step, b_re, b_im):
    g, p, c = b_re.shape
    per_channel = lambda x: jnp.broadcast_to(x.reshape(g, 1, -1), (g, c, p)).reshape(g * c, p)
    rows = lambda x: x.transpose(0, 2, 1).reshape(g * c, p)
    outs = pl.pallas_call(
        _s5_disc_body,
        out_shape=[jax.ShapeDtypeStruct((g * c, p), F32)] * 4,
        name="s5_disc",
    )(per_channel(a_re), per_channel(a_im), per_channel(log_step), rows(b_re), rows(b_im))
    abr, abi, bbr, bbi = [x.reshape(g, c, p) for x in outs]
    return abr[:, 0], abi[:, 0], bbr, bbi


def _s5_body(tt, n_sub, nh, col_w, u_ref, perm_ref, permt_ref, b_ref, are_ref, aim_ref, c_ref, d_ref, wglu_ref,
             bglu_ref, y_ref, st_ref, s_ref):
    @pl.when(pl.program_id(0) == 0)
    def _():
        s_ref[...] = jnp.zeros_like(s_ref)

    nb = SUBLANES
    rows = nb * tt
    width = u_ref.shape[-1]
    kin = width // nh
    nsh = are_ref.shape[-1]
    permt = permt_ref[...]
    permt2 = _cat1(permt, permt)
    us = [u_ref[:, i * tt:(i + 1) * tt, :].reshape(rows, width) for i in range(n_sub)]
    for i in range(n_sub):
        u_tm = jnp.dot(perm_ref[...], us[i].astype(BF16), preferred_element_type=F32).astype(BF16)
        for h in range(nh):
            st_ref[i, h] = jnp.dot(u_tm[:, h * kin:(h + 1) * kin], b_ref[h], preferred_element_type=F32)
    for i in range(n_sub):
        for h in range(nh):
            for cg in range(nsh // col_w):
                re_c = slice(cg * col_w, (cg + 1) * col_w)
                im_c = slice(nsh + cg * col_w, nsh + (cg + 1) * col_w)
                are = jnp.broadcast_to(are_ref[h, :, re_c], (nb, col_w))
                aim = jnp.broadcast_to(aim_ref[h, :, re_c], (nb, col_w))

                def step(t, carry):
                    sre, sim = carry
                    r8 = pl.ds(pl.multiple_of(t * nb, nb), nb)
                    nre = are * sre - aim * sim + st_ref[i, h, r8, re_c]
                    nim = are * sim + aim * sre + st_ref[i, h, r8, im_c]
                    st_ref[i, h, r8, re_c] = nre
                    st_ref[i, h, r8, im_c] = nim
                    return nre, nim

                sre, sim = lax.fori_loop(0, tt, step, (s_ref[h, :, re_c], s_ref[h, :, im_c]), unroll=True)
                s_ref[h, :, re_c] = sre
                s_ref[h, :, im_c] = sim
        y_tm = jnp.concatenate([jnp.dot(st_ref[i, h].astype(BF16), c_ref[h], preferred_element_type=F32)
                                for h in range(nh)], axis=1)
        y = _dot(permt2, _cat0(*_split(y_tm)))
        y = jax.nn.gelu(y + d_ref[...] * us[i])
        y = y * jax.nn.sigmoid(jnp.dot(y.astype(BF16), wglu_ref[...], preferred_element_type=F32) + bglu_ref[...])
        y_ref[:, i * tt:(i + 1) * tt, :] = y.astype(BF16).reshape(nb, tt, width)


def _s5(u3, perm, b_h, are, aim, c_h, d, w_glu, b_glu, tt, n_sub):
    b, l, width = u3.shape
    assert l % (tt * n_sub) == 0 and b == SUBLANES
    nh, _, nsh2 = b_h.shape
    consts = [perm, perm.T, b_h, are, aim, c_h, d, w_glu, b_glu]
    blk = pl.BlockSpec((b, tt * n_sub, width), lambda i: (0, i, 0))
    return pl.pallas_call(
        functools.partial(_s5_body, tt, n_sub, nh, 4 * LANES),
        grid=(l // (tt * n_sub),),
        in_specs=[blk] + [_const_spec(c.shape) for c in consts],
        out_specs=blk,
        out_shape=jax.ShapeDtypeStruct((b, l, width), BF16),
        scratch_shapes=[pltpu.VMEM((n_sub, nh, b * tt, nsh2), F32), pltpu.VMEM((nh, b, nsh2), F32)],
        compiler_params=_params(("arbitrary",)),
        name="s5",
    )(u3, *consts)


def _mix_ffn_body(tm, d, f, cb, x_ref, yr_ref, ys_ref, gate_ref, wbr_ref, wbs_ref, wout_ref, gmix_ref,
                  gpre_ref, wup_ref, cw_ref, cbias_ref, wdown_ref, gpost_ref, o_ref, carry_ref, act_ref):
    @pl.when(pl.program_id(1) == 0)
    def _():
        carry_ref[...] = jnp.zeros_like(carry_ref)

    o_r = jnp.dot(yr_ref[0], wbr_ref[...], preferred_element_type=F32)
    o_s = jnp.dot(ys_ref[0], wbs_ref[...], preferred_element_type=F32)
    mixed = gate_ref[0, :, :d] * o_r + gate_ref[0, :, d:] * o_s
    mixed = jnp.dot(mixed.astype(BF16), wout_ref[...], preferred_element_type=F32)
    x = x_ref[0] + _rms(mixed, gmix_ref[...])

    h = _rms(x, gpre_ref[...]).astype(BF16)
    nb = SUBLANES
    sub = lax.broadcasted_iota(jnp.int32, (nb, 1), 0)

    def conv(cols):
        z = jnp.dot(h, wup_ref[:, cols], preferred_element_type=F32)
        tail = carry_ref[:, cols]
        carry_ref[:, cols] = z[tm - nb:, :]
        out = cbias_ref[:, cols] + cw_ref[CONV_WIDTH - 1:CONV_WIDTH, cols] * z
        for s in range(1, CONV_WIDTH):
            zr = pltpu.roll(z, s, 0)
            top = jnp.where(sub < s, pltpu.roll(tail, s, 0), zr[:nb])
            zs = jnp.concatenate([top, zr[nb:]], axis=0)
            out = out + cw_ref[CONV_WIDTH - 1 - s:CONV_WIDTH - s, cols] * zs
        return out

    for j in range(f // cb):
        gate = conv(slice(j * cb, (j + 1) * cb))
        val = conv(slice(f + j * cb, f + (j + 1) * cb))
        act_ref[:, j * cb:(j + 1) * cb] = (jax.nn.gelu(gate) * val).astype(BF16)
    y = jnp.dot(act_ref[...], wdown_ref[...], preferred_element_type=F32)
    o_ref[0] = x + _rms(y, gpost_ref[...])


def _mix_ffn(x3, yr, ys, gates, wbr, wbs, wout, g_mix, g_pre, w_up, conv_w, conv_b, w_down, g_post, tm):
    b, l, d = x3.shape
    f = w_down.shape[0]
    cb = 2 * LANES
    assert l % tm == 0 and f % cb == 0
    consts = [wbr, wbs, wout, g_mix, g_pre, w_up, conv_w, conv_b, w_down, g_post]
    tile = lambda a: pl.BlockSpec((1, tm, a.shape[-1]), lambda i, j: (i, j, 0))
    acts = [x3, yr, ys, gates]
    return pl.pallas_call(
        functools.partial(_mix_ffn_body, tm, d, f, cb),
        grid=(b, l // tm),
        in_specs=[tile(a) for a in acts] + [_const_spec(c.shape) for c in consts],
        out_specs=tile(x3),
        out_shape=jax.ShapeDtypeStruct((b, l, d), F32),
        scratch_shapes=[pltpu.VMEM((SUBLANES, 2 * f), F32), pltpu.VMEM((tm, f), BF16)],
        compiler_params=_params(("arbitrary", "arbitrary")),
        name="mix_ffn",
    )(*acts, *consts)


def _block_diag(blocks):
    g, r, c = blocks.shape
    eye = jnp.eye(g, dtype=blocks.dtype)
    return (blocks[:, :, None, :] * eye[:, None, :, None]).reshape(g * r, g * c)


def _layer(x, norm_mix_pre, norm_mix_post, norm_ffn_pre, norm_ffn_post, w_in, b_gate, mu, w0, w2, a0, a2, g2,
           k_k, k_a, r_k, lnx_w, lnx_b, s5_a_re, s5_a_im, s5_b_re, s5_b_im, s5_c_re, s5_c_im, s5_d, s5_log_step,
           s5_w_glu, s5_b_glu, w_branch_rwkv, w_branch_s5, w_out, ffn_w_up, ffn_conv_w, ffn_conv_b, ffn_w_down,
           tiles):
    b, l, d = x.shape
    width = w0.shape[0]
    s5_width = s5_d.shape[0]
    n_rwkv = mu.shape[0]
    rank_w, rank_a = w2.shape[0], a2.shape[0]
    assert rank_w == HEAD and rank_a == HEAD and g2.shape[0] == LANES and b == SUBLANES
    rowv = lambda vec: vec.reshape(1, -1)
    p, u, gates = _in_proj(x, rowv(norm_mix_pre), w_in, rowv(b_gate), rowv(mu), n_rwkv, s5_width,
                           tiles["in"])

    wwa = jnp.zeros((LANES, 2 * width), F32).at[:rank_w, :width].set(w2).at[rank_w:, width:].set(a2)
    ones_bd = _block_diag(jnp.ones((MXU_K // HEAD, HEAD, HEAD), F32)).astype(BF16)
    y_rwkv = _rwkv(p, rowv(w0), wwa.astype(BF16), rowv(a0), g2.astype(BF16), rowv(k_k), rowv(k_a), rowv(r_k),
                   rowv(lnx_w), rowv(lnx_b), ones_bd, width, tiles["rwkv"])

    abr, abi, bbr, bbi = _s5_disc(s5_a_re, s5_a_im, s5_log_step, s5_b_re, s5_b_im)
    n_grp = s5_a_re.shape[0]
    gh = LANES // S5_GROUP
    nh = n_grp // gh
    halves = lambda a: a.reshape((nh, gh) + a.shape[1:])
    bd = jax.vmap(_block_diag)
    tr = lambda a: halves(a).transpose(0, 1, 3, 2)
    b_h = jnp.concatenate([bd(halves(bbr)), bd(halves(bbi))], axis=2)
    c_h = jnp.concatenate([bd(tr(s5_c_re)), -bd(tr(s5_c_im))], axis=1)
    tt = tiles["s5"]
    ridx = jnp.arange(b * tt)
    perm = (ridx[None, :] == ((ridx % b) * tt + ridx // b)[:, None]).astype(BF16)
    y_s5 = _s5(u, perm, b_h.astype(BF16), abr.reshape(nh, 1, -1), abi.reshape(nh, 1, -1),
               c_h.astype(BF16), rowv(s5_d), s5_w_glu.astype(BF16), rowv(s5_b_glu), tt, tiles["s5_sub"])

    return _mix_ffn(x, y_rwkv, y_s5, gates, w_branch_rwkv.astype(BF16),
                    w_branch_s5.astype(BF16), w_out.astype(BF16), rowv(norm_mix_post), rowv(norm_ffn_pre),
                    ffn_w_up.astype(BF16), ffn_conv_w, rowv(ffn_conv_b), ffn_w_down.astype(BF16),
                    rowv(norm_ffn_post), tiles["ffn"])


TILES = {"in": 512, "rwkv": 1024, "s5": 32, "s5_sub": 4, "ffn": 512}


def kernel(x, norm_mix_pre, norm_mix_post, norm_ffn_pre, norm_ffn_post, w_in, b_gate, rwkv_shift_mu, rwkv_w0, rwkv_w2, rwkv_a0, rwkv_a2, rwkv_g2, rwkv_k_k, rwkv_k_a, rwkv_r_k, rwkv_lnx_w, rwkv_lnx_b, s5_a_re, s5_a_im, s5_b_re, s5_b_im, s5_c_re, s5_c_im, s5_d, s5_log_step, s5_w_glu, s5_b_glu, w_branch_rwkv, w_branch_s5, w_out, ffn_w_up, ffn_conv_w, ffn_conv_b, ffn_w_down):
    depth = w_in.shape[0]
    for i in range(depth):
        x = _layer(x, norm_mix_pre[i], norm_mix_post[i], norm_ffn_pre[i], norm_ffn_post[i], w_in[i], b_gate[i],
                   rwkv_shift_mu[i], rwkv_w0[i], rwkv_w2[i], rwkv_a0[i], rwkv_a2[i], rwkv_g2[i], rwkv_k_k[i],
                   rwkv_k_a[i], rwkv_r_k[i].reshape(-1), rwkv_lnx_w[i], rwkv_lnx_b[i], s5_a_re[i], s5_a_im[i],
                   s5_b_re[i], s5_b_im[i], s5_c_re[i], s5_c_im[i], s5_d[i], s5_log_step[i], s5_w_glu[i],
                   s5_b_glu[i], w_branch_rwkv[i], w_branch_s5[i], w_out[i], ffn_w_up[i], ffn_conv_w[i],
                   ffn_conv_b[i], ffn_w_down[i], TILES)
    return x
```

```python
import functools
import math

import jax
import jax.numpy as jnp
from jax import lax
from jax.experimental import pallas as pl
from jax.experimental.pallas import tpu as pltpu

F32 = jnp.float32
BF16 = jnp.bfloat16

NORM_EPS = 1e-6
LNX_EPS = 64e-5
HEAD = 64
CHUNK = 64
WKV_GROUP = 4
FILL_PER_STAGE = 3
LANES = 128
SUBLANES = 8
S5_GROUP = 16
S5_STATE = 64
CONV_WIDTH = 3
ROW_SPLIT = 2
MXU_K = 256
VMEM_LIMIT = 56 * 1024 * 1024


def _rms(x, g):
    return x * lax.rsqrt(jnp.mean(x * x, axis=-1, keepdims=True) + NORM_EPS) * g


def _split(x):
    hi = x.astype(BF16)
    lo = (x - hi.astype(F32)).astype(BF16)
    return hi, lo


_NN = (((1,), (0,)), ((), ()))
_NT = (((1,), (1,)), ((), ()))
_TN = (((0,), (0,)), ((), ()))


def _dot(a, b, dims=_NN):
    return lax.dot_general(a, b, dims, preferred_element_type=F32)


def _mm(a, b):
    return _dot(a.astype(BF16), b.astype(BF16))


def _cat0(*xs):
    return jnp.concatenate(xs, axis=0)


def _cat1(*xs):
    return jnp.concatenate(xs, axis=1)


def _const_spec(shape):
    nd = len(shape)
    return pl.BlockSpec(shape, lambda *_: (0,) * nd)


def _params(sem):
    return pltpu.CompilerParams(dimension_semantics=sem, vmem_limit_bytes=VMEM_LIMIT)


def _in_proj_body(tm, n_rwkv, n_s5, x_ref, g_ref, w32_ref, bg_ref, mu_ref, p_ref, u_ref, gate_ref, carry_ref,
                  w_ref):
    @pl.when((pl.program_id(0) == 0) & (pl.program_id(1) == 0))
    def _():
        for c0 in range(0, w_ref.shape[1], MXU_K):
            w_ref[:, c0:c0 + MXU_K] = w32_ref[:, c0:c0 + MXU_K].astype(BF16)

    @pl.when(pl.program_id(1) == 0)
    def _():
        carry_ref[...] = jnp.zeros_like(carry_ref)

    blocks = [slice(i * tm // ROW_SPLIT, (i + 1) * tm // ROW_SPLIT) for i in range(ROW_SPLIT)]
    ps = []
    for r in blocks:
        h = _rms(x_ref[0, r], g_ref[...]).astype(BF16)
        ps.append(jnp.dot(h, w_ref[:, :n_rwkv], preferred_element_type=F32))
        u_ref[0, r] = jnp.dot(h, w_ref[:, n_rwkv:n_rwkv + n_s5], preferred_element_type=F32)
        gate_ref[0, r] = jax.nn.sigmoid(
            jnp.dot(h, w_ref[:, n_rwkv + n_s5:], preferred_element_type=F32) + bg_ref[...]).astype(BF16)
    p = _cat0(*ps)
    row = lax.broadcasted_iota(jnp.int32, (tm, 1), 0)
    prev = jnp.where(row == 0, carry_ref[...], pltpu.roll(p, 1, 0))
    carry_ref[...] = p[tm - 1:tm, :]
    p_ref[0] = p + (prev - p) * mu_ref[...]


def _in_proj(x3, g, w_in, b_gate, mu, n_rwkv, n_s5, tm):
    b, l, d = x3.shape
    assert l % tm == 0
    n_in = w_in.shape[1]
    n_gate = n_in - n_rwkv - n_s5
    tile = lambda n: pl.BlockSpec((1, tm, n), lambda i, j: (i, j, 0))
    return pl.pallas_call(
        functools.partial(_in_proj_body, tm, n_rwkv, n_s5),
        grid=(b, l // tm),
        in_specs=[tile(d), _const_spec((1, d)), _const_spec((d, n_in)), _const_spec((1, n_gate)),
                  _const_spec((1, n_rwkv))],
        out_specs=[tile(n_rwkv), tile(n_s5), tile(n_gate)],
        out_shape=[jax.ShapeDtypeStruct((b, l, n_rwkv), F32), jax.ShapeDtypeStruct((b, l, n_s5), F32),
                   jax.ShapeDtypeStruct((b, l, n_gate), BF16)],
        scratch_shapes=[pltpu.VMEM((1, n_rwkv), F32), pltpu.VMEM((d, n_in), BF16)],
        compiler_params=_params(("arbitrary", "arbitrary")),
        name="in_proj",
    )(x3, g, w_in, b_gate, mu)


def _mstack(x, head_masks):
    z = jnp.zeros_like(x)
    return jnp.concatenate([jnp.where(m, x, z) for m in head_masks], axis=0)


def _wkv_stages(fetch, chunk_ids, hs, emit, masks):
    pair_masks, strict, incl, eye_mask = masks
    eye = eye_mask.astype(F32)
    c = CHUNK
    ms = lambda x: _mstack(x, pair_masks)
    b16 = lambda x: x.astype(BF16)
    n_pairs = len(hs)
    ids = [(k, j) for k in range(len(chunk_ids)) for j in range(n_pairs)]
    get = lambda name: [fetch(name, chunk_ids[k], j) for k, j in ids]
    n = range(len(ids))
    at, rt, bt, kt = get("at"), [b16(x) for x in get("rt")], get("bt"), get("kt")
    g = [_dot(_cat0(at[i], rt[i]), _cat0(ms(bt[i]), ms(kt[i])), _NT) for i in n]
    yield
    a_ab = [jnp.where(strict, x[:c, :LANES], 0.0) for x in g]
    a_ak = [b16(jnp.where(strict, x[:c, LANES:], 0.0)) for x in g]
    a_rbk = [b16(_cat1(jnp.where(incl, x[c:, :LANES], 0.0), jnp.where(incl, x[c:, LANES:], 0.0))) for x in g]
    ab16 = [b16(x) for x in a_ab]
    q = [b16(_dot(ab16[i], ms(ab16[i]))) for i in n]
    t_inv = [eye + x for x in a_ab]
    yield
    n_sq = int(math.log2(c)) - 1
    for s in range(n_sq - 1):
        tq = [_dot(_cat0(b16(t_inv[i]), q[i]), ms(q[i])) for i in n]
        t_inv = [t_inv[i] + tq[i][:c] for i in n]
        q = [b16(tq[i][c:]) for i in n]
        yield
    t16 = [b16(t_inv[i]) for i in n]
    t_inv = [b16(t_inv[i] + _dot(t16[i], ms(q[i]))) for i in n]
    msv = [ms(x) for x in get("v")]
    w0 = [b16(_dot(a_ak[i], msv[i])) for i in n]
    yield
    at = get("at")
    x1 = [_dot(t_inv[i], _cat1(ms(at[i]), ms(w0[i]))) for i in n]
    ap = [b16(x[:, :LANES]) for x in x1]
    u0 = [b16(x[:, LANES:]) for x in x1]
    yield
    zero = jnp.zeros((2 * c, LANES), BF16)
    ry = [_dot(a_rbk[i], _cat0(_cat1(ms(ap[i]), ms(u0[i])), _cat1(zero, msv[i]))) for i in n]
    yield
    bh, kh, v, rt32, pc = [get(name) for name in ("bh", "kh", "v", "rt", "pc")]
    mh = [_dot(_cat0(bh[i], kh[i]), _cat0(_cat1(ap[i], u0[i]), _cat1(zero[:c], v[i])), _TN) for i in n]
    rp = [b16(rt32[i] + ry[i][:, :LANES]) for i in n]
    yield
    blocks = lambda x: jnp.where(pair_masks[0], x[:c], x[c:])
    m = [b16(blocks(mh[i][:, :LANES]) + jnp.where(eye_mask, pc[i], 0.0)) for i in n]
    hadd = [blocks(mh[i][:, LANES:]) for i in n]

    def apply_state(k):
        for j in range(n_pairs):
            i = k * n_pairs + j
            yh = _dot(_cat0(rp[i], m[i]), ms(b16(hs[j])))
            emit(chunk_ids[k], j, yh[:c] + ry[i][:, LANES:])
            hs[j] = yh[c:] + hadd[i]

    return [functools.partial(apply_state, k) for k in range(len(chunk_ids))]


def _drive(stages, fillers, per_stage):
    while True:
        try:
            next(stages)
        except StopIteration as done:
            return done.value
        for _ in range(per_stage):
            if fillers:
                fillers.pop(0)()


def _rwkv_body(tl, width, p_ref, w0_ref, wwa_ref, a0_ref, g2_ref, kk_ref, ka_ref, rk_ref, lnw_ref, lnb_ref,
               ones_ref, y_ref, h_ref, pre_ref, twa_ref, g_ref, bonus_ref, ss_ref, rt_ref, ops_ref, pc_ref,
               yacc_ref):
    @pl.when(pl.program_id(1) == 0)
    def _():
        h_ref[...] = jnp.zeros_like(h_ref)

    c = CHUNK
    w3 = 3 * width
    n_pairs = width // LANES
    r_cols, k_cols, v_cols = slice(0, width), slice(width, 2 * width), slice(2 * width, w3)
    lane = lax.broadcasted_iota(jnp.int32, (1, LANES), 1)
    m0 = lane < HEAD
    ones = ones_ref[...]
    nk = ones.shape[0]
    segsum = lambda x: _cat1(*[_mm(x[:, i:i + nk], ones) for i in range(0, width, nk)])
    rowc = lax.broadcasted_iota(jnp.int32, (c, LANES), 0)
    colc = lax.broadcasted_iota(jnp.int32, (c, LANES), 1) % HEAD
    masks = ([m0, jnp.logical_not(m0)], colc < rowc, colc <= rowc, colc == rowc)
    ltri = (lax.broadcasted_iota(jnp.int32, (c, c), 1) <= lax.broadcasted_iota(jnp.int32, (c, c), 0)).astype(BF16)
    ltri2 = _cat1(ltri, ltri)

    def low_rank(rows):
        wa = p_ref[0, rows, w3:w3 + LANES]
        twa_ref[rows, :] = _mm(jnp.where(m0, jnp.tanh(wa), wa), wwa_ref[...])
        g_ref[rows, :] = _mm(jax.nn.sigmoid(p_ref[0, rows, w3 + LANES:]), g2_ref[...])

    def elementwise(rows):
        pre_ref[3, rows, :] = -math.exp(-0.5) * jax.nn.sigmoid(w0_ref[...] + twa_ref[rows, :width])
        a = jax.nn.sigmoid(a0_ref[...] + twa_ref[rows, width:])
        k = p_ref[0, rows, k_cols]
        k2 = k * (1.0 + (a - 1.0) * ka_ref[...])
        pre_ref[0, rows, :] = k2
        pre_ref[1, rows, :] = k * kk_ref[...]
        pre_ref[2, rows, :] = a
        bonus_ref[rows, :] = p_ref[0, rows, r_cols] * k2 * rk_ref[...]

    def head_sums(rows):
        kk = pre_ref[1, rows, :]
        ss_ref[rows, :] = segsum(kk * kk)
        bonus_ref[rows, :] = segsum(bonus_ref[rows, :])

    def normalise(rows):
        kk = pre_ref[1, rows, :] * lax.rsqrt(jnp.maximum(ss_ref[rows, :], 1e-24))
        pre_ref[2, rows, :] = kk * pre_ref[2, rows, :]
        pre_ref[1, rows, :] = -kk
        bonus_ref[rows, :] = bonus_ref[rows, :] * p_ref[0, rows, v_cols]

    def decay_scale(ci):
        rows = slice(ci * c, (ci + 1) * c)
        lwc = pre_ref[3, rows, :]
        cum = _dot(ltri2, _cat0(*_split(lwc)))
        pw = jnp.exp(cum)
        pinv = jnp.exp(-cum)
        kt = pre_ref[0, rows, :] * pinv
        bt = pre_ref[2, rows, :] * pinv
        pc = pw[c - 1:c, :]
        rt_ref[rows, :] = p_ref[0, rows, r_cols] * pw
        for i, x in enumerate((pre_ref[1, rows, :] * jnp.exp(cum - lwc), kt, bt, bt * pc, kt * pc,
                               p_ref[0, rows, v_cols])):
            ops_ref[i, rows, :] = x.astype(BF16)
        pc_ref[ci:ci + 1, :] = pc

    op_names = ("at", "kt", "bt", "bh", "kh", "v")

    def fetch(name, ci, j):
        rows, lanes = slice(ci * c, (ci + 1) * c), slice(j * LANES, (j + 1) * LANES)
        if name == "rt":
            return rt_ref[rows, lanes]
        if name == "pc":
            return pc_ref[ci:ci + 1, lanes]
        return ops_ref[op_names.index(name), rows, lanes]

    def centre(rows):
        y = yacc_ref[rows, :]
        yacc_ref[rows, :] = y - segsum(y) * (1.0 / HEAD)

    def group_norm(rows):
        yc = yacc_ref[rows, :]
        var = segsum(yc * yc) * (1.0 / HEAD)
        yn = yc * lax.rsqrt(var + LNX_EPS) * lnw_ref[...] + lnb_ref[...]
        y_ref[0, rows, :] = ((yn + bonus_ref[rows, :]) * g_ref[rows, :]).astype(BF16)

    gc = WKV_GROUP
    n_groups = tl // (c * gc)
    half = c * gc // 2

    def prepare(gi):
        r0 = gi * gc * c
        blocks = [slice(r0 + i * c, r0 + (i + 1) * c) for i in range(gc)]
        halves = [slice(r0, r0 + half), slice(r0 + half, r0 + 2 * half)]
        return ([functools.partial(low_rank, h) for h in halves] + [functools.partial(elementwise, b) for b in blocks]
                + [functools.partial(head_sums, h) for h in halves] + [functools.partial(normalise, b) for b in blocks]
                + [functools.partial(decay_scale, ci) for ci in range(gi * gc, (gi + 1) * gc)])

    def finish(gi):
        r0 = gi * gc * c
        halves = [slice(r0, r0 + half), slice(r0 + half, r0 + 2 * half)]
        return [functools.partial(f, h) for f in (centre, group_norm) for h in halves]

    def emit(ci, j, y):
        yacc_ref[ci * c:(ci + 1) * c, j * LANES:(j + 1) * LANES] = y

    for thunk in prepare(0):
        thunk()
    hs = [h_ref[j] for j in range(n_pairs)]
    state_steps = []
    for gi in range(n_groups):
        prep = prepare(gi + 1) if gi + 1 < n_groups else []
        fillers = []
        for step in state_steps:
            fillers += [step] + prep[:2]
            prep = prep[2:]
        fillers += prep + (finish(gi - 1) if gi > 0 else [])
        chunk_ids = list(range(gi * gc, (gi + 1) * gc))
        state_steps = _drive(_wkv_stages(fetch, chunk_ids, hs, emit, masks), fillers, FILL_PER_STAGE)
        for thunk in fillers:
            thunk()
    for thunk in state_steps + finish(n_groups - 1):
        thunk()
    for j in range(n_pairs):
        h_ref[j] = hs[j]


def _rwkv(p3, w0, wwa, a0, g2, k_k, k_a, r_k, lnx_w, lnx_b, ones_bd, width, tl):
    b, l, n_rwkv = p3.shape
    assert l % tl == 0 and tl % (CHUNK * WKV_GROUP) == 0 and (tl // CHUNK) % SUBLANES == 0
    n_pairs = width // LANES
    consts = [w0, wwa, a0, g2, k_k, k_a, r_k, lnx_w, lnx_b, ones_bd]
    rows = lambda n=width: pltpu.VMEM((tl, n), F32)
    return pl.pallas_call(
        functools.partial(_rwkv_body, tl, width),
        grid=(b, l // tl),
        in_specs=[pl.BlockSpec((1, tl, n_rwkv), lambda i, j: (i, j, 0))] + [_const_spec(c.shape) for c in consts],
        out_specs=pl.BlockSpec((1, tl, width), lambda i, j: (i, j, 0)),
        out_shape=jax.ShapeDtypeStruct((b, l, width), BF16),
        scratch_shapes=[pltpu.VMEM((n_pairs, HEAD, LANES), F32), pltpu.VMEM((4, tl, width), F32),
                        rows(2 * width), rows(), rows(), rows(), rows(), pltpu.VMEM((6, tl, width), BF16),
                        pltpu.VMEM((tl // CHUNK, width), F32), rows()],
        compiler_params=_params(("arbitrary", "arbitrary")),
        name="rwkv7",
    )(p3, *consts)


def _s5_disc_body(are_ref, aim_ref, dt_ref, bre_ref, bim_ref, abr_ref, abi_ref, bbr_ref, bbi_ref):
    lre, lim, dt = are_ref[...], aim_ref[...], jnp.exp(dt_ref[...])
    mag = jnp.exp(lre * dt)
    abr = mag * jnp.cos(lim * dt)
    abi = mag * jnp.sin(lim * dt)
    inv = 1.0 / (lre * lre + lim * lim)
    nr, ni = abr - 1.0, abi
    cr = (nr * lre + ni * lim) * inv
    ci = (ni * lre - nr * lim) * inv
    bre, bim = bre_ref[...], bim_ref[...]
    abr_ref[...] = abr
    abi_ref[...] = abi
    bbr_ref[...] = cr * bre - ci * bim
    bbi_ref[...] = cr * bim + ci * bre


def _s5_disc(a_re, a_im, log_step, b_re, b_im):
    g, p, c = b_re.shape
    per_channel = lambda x: jnp.broadcast_to(x.reshape(g, 1, -1), (g, c, p)).reshape(g * c, p)
    rows = lambda x: x.transpose(0, 2, 1).reshape(g * c, p)
    outs = pl.pallas_call(
        _s5_disc_body,
        out_shape=[jax.ShapeDtypeStruct((g * c, p), F32)] * 4,
        name="s5_disc",
    )(per_channel(a_re), per_channel(a_im), per_channel(log_step), rows(b_re), rows(b_im))
    abr, abi, bbr, bbi = [x.reshape(g, c, p) for x in outs]
    return abr[:, 0], abi[:, 0], bbr, bbi


def _s5_body(tt, n_sub, nh, col_w, u_ref, perm_ref, permt_ref, b_ref, are_ref, aim_ref, c_ref, d_ref, wglu_ref,
             bglu_ref, y_ref, st_ref, s_ref):
    @pl.when(pl.program_id(0) == 0)
    def _():
        s_ref[...] = jnp.zeros_like(s_ref)

    nb = SUBLANES
    rows = nb * tt
    width = u_ref.shape[-1]
    kin = width // nh
    nsh = are_ref.shape[-1]
    permt = permt_ref[...]
    permt2 = _cat1(permt, permt)
    us = [u_ref[:, i * tt:(i + 1) * tt, :].reshape(rows, width) for i in range(n_sub)]
    for i in range(n_sub):
        u_tm = jnp.dot(perm_ref[...], us[i].astype(BF16), preferred_element_type=F32).astype(BF16)
        for h in range(nh):
            st_ref[i, h] = jnp.dot(u_tm[:, h * kin:(h + 1) * kin], b_ref[h], preferred_element_type=F32)
    for i in range(n_sub):
        for h in range(nh):
            for cg in range(nsh // col_w):
                re_c = slice(cg * col_w, (cg + 1) * col_w)
                im_c = slice(nsh + cg * col_w, nsh + (cg + 1) * col_w)
                are = jnp.broadcast_to(are_ref[h, :, re_c], (nb, col_w))
                aim = jnp.broadcast_to(aim_ref[h, :, re_c], (nb, col_w))

                def step(t, carry):
                    sre, sim = carry
                    r8 = pl.ds(pl.multiple_of(t * nb, nb), nb)
                    nre = are * sre - aim * sim + st_ref[i, h, r8, re_c]
                    nim = are * sim + aim * sre + st_ref[i, h, r8, im_c]
                    st_ref[i, h, r8, re_c] = nre
                    st_ref[i, h, r8, im_c] = nim
                    return nre, nim

                sre, sim = lax.fori_loop(0, tt, step, (s_ref[h, :, re_c], s_ref[h, :, im_c]), unroll=True)
                s_ref[h, :, re_c] = sre
                s_ref[h, :, im_c] = sim
        y_tm = jnp.concatenate([jnp.dot(st_ref[i, h].astype(BF16), c_ref[h], preferred_element_type=F32)
                                for h in range(nh)], axis=1)
        y = _dot(permt2, _cat0(*_split(y_tm)))
        y = jax.nn.gelu(y + d_ref[...] * us[i])
        y = y * jax.nn.sigmoid(jnp.dot(y.astype(BF16), wglu_ref[...], preferred_element_type=F32) + bglu_ref[...])
        y_ref[:, i * tt:(i + 1) * tt, :] = y.astype(BF16).reshape(nb, tt, width)


def _s5(u3, perm, b_h, are, aim, c_h, d, w_glu, b_glu, tt, n_sub):
    b, l, width = u3.shape
    assert l % (tt * n_sub) == 0 and b == SUBLANES
    nh, _, nsh2 = b_h.shape
    consts = [perm, perm.T, b_h, are, aim, c_h, d, w_glu, b_glu]
    blk = pl.BlockSpec((b, tt * n_sub, width), lambda i: (0, i, 0))
    return pl.pallas_call(
        functools.partial(_s5_body, tt, n_sub, nh, 4 * LANES),
        grid=(l // (tt * n_sub),),
        in_specs=[blk] + [_const_spec(c.shape) for c in consts],
        out_specs=blk,
        out_shape=jax.ShapeDtypeStruct((b, l, width), BF16),
        scratch_shapes=[pltpu.VMEM((n_sub, nh, b * tt, nsh2), F32), pltpu.VMEM((nh, b, nsh2), F32)],
        compiler_params=_params(("arbitrary",)),
        name="s5",
    )(u3, *consts)


def _mix_ffn_body(tm, d, f, cb, x_ref, yr_ref, ys_ref, gate_ref, wbr_ref, wbs_ref, wout_ref, gmix_ref,
                  gpre_ref, wup_ref, cw_ref, cbias_ref, wdown_ref, gpost_ref, o_ref, carry_ref, act_ref):
    @pl.when(pl.program_id(1) == 0)
    def _():
        carry_ref[...] = jnp.zeros_like(carry_ref)

    blocks = [slice(i * tm // ROW_SPLIT, (i + 1) * tm // ROW_SPLIT) for i in range(ROW_SPLIT)]
    mixed = [gate_ref[0, r, :d] * jnp.dot(yr_ref[0, r], wbr_ref[...], preferred_element_type=F32)
             + gate_ref[0, r, d:] * jnp.dot(ys_ref[0, r], wbs_ref[...], preferred_element_type=F32) for r in blocks]
    mixed = [jnp.dot(m.astype(BF16), wout_ref[...], preferred_element_type=F32) for m in mixed]
    xs = [x_ref[0, r] + _rms(m, gmix_ref[...]) for r, m in zip(blocks, mixed)]

    h = _cat0(*[_rms(x, gpre_ref[...]).astype(BF16) for x in xs])
    nb = SUBLANES
    sub = lax.broadcasted_iota(jnp.int32, (nb, 1), 0)

    def conv(cols):
        z = jnp.dot(h, wup_ref[:, cols], preferred_element_type=F32)
        tail = carry_ref[:, cols]
        carry_ref[:, cols] = z[tm - nb:, :]
        out = cbias_ref[:, cols] + cw_ref[CONV_WIDTH - 1:CONV_WIDTH, cols] * z
        for s in range(1, CONV_WIDTH):
            zr = pltpu.roll(z, s, 0)
            top = jnp.where(sub < s, pltpu.roll(tail, s, 0), zr[:nb])
            zs = jnp.concatenate([top, zr[nb:]], axis=0)
            out = out + cw_ref[CONV_WIDTH - 1 - s:CONV_WIDTH - s, cols] * zs
        return out

    for j in range(f // cb):
        gate = conv(slice(j * cb, (j + 1) * cb))
        val = conv(slice(f + j * cb, f + (j + 1) * cb))
        act_ref[:, j * cb:(j + 1) * cb] = (jax.nn.gelu(gate) * val).astype(BF16)
    ys = [jnp.dot(act_ref[r, :], wdown_ref[...], preferred_element_type=F32) for r in blocks]
    for r, x, y in zip(blocks, xs, ys):
        o_ref[0, r] = x + _rms(y, gpost_ref[...])


def _mix_ffn(x3, yr, ys, gates, wbr, wbs, wout, g_mix, g_pre, w_up, conv_w, conv_b, w_down, g_post, tm):
    b, l, d = x3.shape
    f = w_down.shape[0]
    cb = 2 * LANES
    assert l % tm == 0 and f % cb == 0
    consts = [wbr, wbs, wout, g_mix, g_pre, w_up, conv_w, conv_b, w_down, g_post]
    tile = lambda a: pl.BlockSpec((1, tm, a.shape[-1]), lambda i, j: (i, j, 0))
    acts = [x3, yr, ys, gates]
    return pl.pallas_call(
        functools.partial(_mix_ffn_body, tm, d, f, cb),
        grid=(b, l // tm),
        in_specs=[tile(a) for a in acts] + [_const_spec(c.shape) for c in consts],
        out_specs=tile(x3),
        out_shape=jax.ShapeDtypeStruct((b, l, d), F32),
        scratch_shapes=[pltpu.VMEM((SUBLANES, 2 * f), F32), pltpu.VMEM((tm, f), BF16)],
        compiler_params=_params(("arbitrary", "arbitrary")),
        name="mix_ffn",
    )(*acts, *consts)


def _block_diag(blocks):
    g, r, c = blocks.shape
    eye = jnp.eye(g, dtype=blocks.dtype)
    return (blocks[:, :, None, :] * eye[:, None, :, None]).reshape(g * r, g * c)


def _layer(x, norm_mix_pre, norm_mix_post, norm_ffn_pre, norm_ffn_post, w_in, b_gate, mu, w0, w2, a0, a2, g2,
           k_k, k_a, r_k, lnx_w, lnx_b, s5_a_re, s5_a_im, s5_b_re, s5_b_im, s5_c_re, s5_c_im, s5_d, s5_log_step,
           s5_w_glu, s5_b_glu, w_branch_rwkv, w_branch_s5, w_out, ffn_w_up, ffn_conv_w, ffn_conv_b, ffn_w_down,
           tiles):
    b, l, d = x.shape
    width = w0.shape[0]
    s5_width = s5_d.shape[0]
    n_rwkv = mu.shape[0]
    rank_w, rank_a = w2.shape[0], a2.shape[0]
    assert rank_w == HEAD and rank_a == HEAD and g2.shape[0] == LANES and b == SUBLANES
    rowv = lambda vec: vec.reshape(1, -1)
    p, u, gates = _in_proj(x, rowv(norm_mix_pre), w_in, rowv(b_gate), rowv(mu), n_rwkv, s5_width,
                           tiles["in"])

    wwa = jnp.zeros((LANES, 2 * width), F32).at[:rank_w, :width].set(w2).at[rank_w:, width:].set(a2)
    ones_bd = _block_diag(jnp.ones((MXU_K // HEAD, HEAD, HEAD), F32)).astype(BF16)
    y_rwkv = _rwkv(p, rowv(w0), wwa.astype(BF16), rowv(a0), g2.astype(BF16), rowv(k_k), rowv(k_a), rowv(r_k),
                   rowv(lnx_w), rowv(lnx_b), ones_bd, width, tiles["rwkv"])

    abr, abi, bbr, bbi = _s5_disc(s5_a_re, s5_a_im, s5_log_step, s5_b_re, s5_b_im)
    n_grp = s5_a_re.shape[0]
    gh = LANES // S5_GROUP
    nh = n_grp // gh
    halves = lambda a: a.reshape((nh, gh) + a.shape[1:])
    bd = jax.vmap(_block_diag)
    tr = lambda a: halves(a).transpose(0, 1, 3, 2)
    b_h = jnp.concatenate([bd(halves(bbr)), bd(halves(bbi))], axis=2)
    c_h = jnp.concatenate([bd(tr(s5_c_re)), -bd(tr(s5_c_im))], axis=1)
    tt = tiles["s5"]
    ridx = jnp.arange(b * tt)
    perm = (ridx[None, :] == ((ridx % b) * tt + ridx // b)[:, None]).astype(BF16)
    y_s5 = _s5(u, perm, b_h.astype(BF16), abr.reshape(nh, 1, -1), abi.reshape(nh, 1, -1),
               c_h.astype(BF16), rowv(s5_d), s5_w_glu.astype(BF16), rowv(s5_b_glu), tt, tiles["s5_sub"])

    return _mix_ffn(x, y_rwkv, y_s5, gates, w_branch_rwkv.astype(BF16),
                    w_branch_s5.astype(BF16), w_out.astype(BF16), rowv(norm_mix_post), rowv(norm_ffn_pre),
                    ffn_w_up.astype(BF16), ffn_conv_w, rowv(ffn_conv_b), ffn_w_down.astype(BF16),
                    rowv(norm_ffn_post), tiles["ffn"])


TILES = {"in": 512, "rwkv": 1024, "s5": 32, "s5_sub": 4, "ffn": 512}


def kernel(x, norm_mix_pre, norm_mix_post, norm_ffn_pre, norm_ffn_post, w_in, b_gate, rwkv_shift_mu, rwkv_w0, rwkv_w2, rwkv_a0, rwkv_a2, rwkv_g2, rwkv_k_k, rwkv_k_a, rwkv_r_k, rwkv_lnx_w, rwkv_lnx_b, s5_a_re, s5_a_im, s5_b_re, s5_b_im, s5_c_re, s5_c_im, s5_d, s5_log_step, s5_w_glu, s5_b_glu, w_branch_rwkv, w_branch_s5, w_out, ffn_w_up, ffn_conv_w, ffn_conv_b, ffn_w_down):
    depth = w_in.shape[0]
    for i in range(depth):
        x = _layer(x, norm_mix_pre[i], norm_mix_post[i], norm_ffn_pre[i], norm_ffn_post[i], w_in[i], b_gate[i],
                   rwkv_shift_mu[i], rwkv_w0[i], rwkv_w2[i], rwkv_a0[i], rwkv_a2[i], rwkv_g2[i], rwkv_k_k[i],
                   rwkv_k_a[i], rwkv_r_k[i].reshape(-1), rwkv_lnx_w[i], rwkv_lnx_b[i], s5_a_re[i], s5_a_im[i],
                   s5_b_re[i], s5_b_im[i], s5_c_re[i], s5_c_im[i], s5_d[i], s5_log_step[i], s5_w_glu[i],
                   s5_b_glu[i], w_branch_rwkv[i], w_branch_s5[i], w_out[i], ffn_w_up[i], ffn_conv_w[i],
                   ffn_conv_b[i], ffn_w_down[i], TILES)
    return x
```

```python
import functools
import math

import jax
import jax.numpy as jnp
from jax import lax
from jax.experimental import pallas as pl
from jax.experimental.pallas import tpu as pltpu

F32 = jnp.float32
BF16 = jnp.bfloat16

NORM_EPS = 1e-6
LNX_EPS = 64e-5
HEAD = 64
CHUNK = 64
WKV_GROUP = 4
FILL_PER_STAGE = 3
LANES = 128
SUBLANES = 8
S5_GROUP = 16
S5_STATE = 64
CONV_WIDTH = 3
ROW_SPLIT = 2
MXU_K = 256
VMEM_LIMIT = 56 * 1024 * 1024


def _rms(x, g):
    return x * lax.rsqrt(jnp.mean(x * x, axis=-1, keepdims=True) + NORM_EPS) * g


def _split(x):
    hi = x.astype(BF16)
    lo = (x - hi.astype(F32)).astype(BF16)
    return hi, lo


_NN = (((1,), (0,)), ((), ()))
_NT = (((1,), (1,)), ((), ()))
_TN = (((0,), (0,)), ((), ()))


def _dot(a, b, dims=_NN):
    return lax.dot_general(a, b, dims, preferred_element_type=F32)


def _mm(a, b):
    return _dot(a.astype(BF16), b.astype(BF16))


def _cat0(*xs):
    return jnp.concatenate(xs, axis=0)


def _cat1(*xs):
    return jnp.concatenate(xs, axis=1)


def _const_spec(shape):
    nd = len(shape)
    return pl.BlockSpec(shape, lambda *_: (0,) * nd)


def _params(sem):
    return pltpu.CompilerParams(dimension_semantics=sem, vmem_limit_bytes=VMEM_LIMIT)


def _in_proj_body(tm, n_rwkv, n_s5, x_ref, g_ref, w32_ref, bg_ref, mu_ref, p_ref, u_ref, gate_ref, carry_ref,
                  w_ref):
    @pl.when((pl.program_id(0) == 0) & (pl.program_id(1) == 0))
    def _():
        for c0 in range(0, w_ref.shape[1], MXU_K):
            w_ref[:, c0:c0 + MXU_K] = w32_ref[:, c0:c0 + MXU_K].astype(BF16)

    @pl.when(pl.program_id(1) == 0)
    def _():
        carry_ref[...] = jnp.zeros_like(carry_ref)

    blocks = [slice(i * tm // ROW_SPLIT, (i + 1) * tm // ROW_SPLIT) for i in range(ROW_SPLIT)]
    ps = []
    for r in blocks:
        h = _rms(x_ref[0, r], g_ref[...]).astype(BF16)
        ps.append(jnp.dot(h, w_ref[:, :n_rwkv], preferred_element_type=F32))
        u_ref[0, r] = jnp.dot(h, w_ref[:, n_rwkv:n_rwkv + n_s5], preferred_element_type=F32)
        gate_ref[0, r] = jax.nn.sigmoid(
            jnp.dot(h, w_ref[:, n_rwkv + n_s5:], preferred_element_type=F32) + bg_ref[...]).astype(BF16)
    p = _cat0(*ps)
    row = lax.broadcasted_iota(jnp.int32, (tm, 1), 0)
    prev = jnp.where(row == 0, carry_ref[...], pltpu.roll(p, 1, 0))
    carry_ref[...] = p[tm - 1:tm, :]
    p_ref[0] = p + (prev - p) * mu_ref[...]


def _in_proj(x3, g, w_in, b_gate, mu, n_rwkv, n_s5, tm):
    b, l, d = x3.shape
    assert l % tm == 0
    n_in = w_in.shape[1]
    n_gate = n_in - n_rwkv - n_s5
    tile = lambda n: pl.BlockSpec((1, tm, n), lambda i, j: (i, j, 0))
    return pl.pallas_call(
        functools.partial(_in_proj_body, tm, n_rwkv, n_s5),
        grid=(b, l // tm),
        in_specs=[tile(d), _const_spec((1, d)), _const_spec((d, n_in)), _const_spec((1, n_gate)),
                  _const_spec((1, n_rwkv))],
        out_specs=[tile(n_rwkv), tile(n_s5), tile(n_gate)],
        out_shape=[jax.ShapeDtypeStruct((b, l, n_rwkv), F32), jax.ShapeDtypeStruct((b, l, n_s5), F32),
                   jax.ShapeDtypeStruct((b, l, n_gate), BF16)],
        scratch_shapes=[pltpu.VMEM((1, n_rwkv), F32), pltpu.VMEM((d, n_in), BF16)],
        compiler_params=_params(("arbitrary", "arbitrary")),
        name="in_proj",
    )(x3, g, w_in, b_gate, mu)


def _mstack(x, head_masks):
    z = jnp.zeros_like(x)
    return jnp.concatenate([jnp.where(m, x, z) for m in head_masks], axis=0)


def _wkv_stages(fetch, chunk_ids, hs, emit, masks):
    pair_masks, strict, incl, eye_mask = masks
    eye = eye_mask.astype(F32)
    c = CHUNK
    ms = lambda x: _mstack(x, pair_masks)
    b16 = lambda x: x.astype(BF16)
    n_pairs = len(hs)
    ids = [(k, j) for k in range(len(chunk_ids)) for j in range(n_pairs)]
    get = lambda name: [fetch(name, chunk_ids[k], j) for k, j in ids]
    n = range(len(ids))
    at, rt, bt, kt = get("at"), [b16(x) for x in get("rt")], get("bt"), get("kt")
    g = [_dot(_cat0(at[i], rt[i]), _cat0(ms(bt[i]), ms(kt[i])), _NT) for i in n]
    yield
    a_ab = [jnp.where(strict, x[:c, :LANES], 0.0) for x in g]
    a_ak = [b16(jnp.where(strict, x[:c, LANES:], 0.0)) for x in g]
    a_rbk = [b16(_cat1(jnp.where(incl, x[c:, :LANES], 0.0), jnp.where(incl, x[c:, LANES:], 0.0))) for x in g]
    ab16 = [b16(x) for x in a_ab]
    q = [b16(_dot(ab16[i], ms(ab16[i]))) for i in n]
    t_inv = [eye + x for x in a_ab]
    yield
    n_sq = int(math.log2(c)) - 1
    for s in range(n_sq - 1):
        tq = [_dot(_cat0(b16(t_inv[i]), q[i]), ms(q[i])) for i in n]
        t_inv = [t_inv[i] + tq[i][:c] for i in n]
        q = [b16(tq[i][c:]) for i in n]
        yield
    t16 = [b16(t_inv[i]) for i in n]
    t_inv = [b16(t_inv[i] + _dot(t16[i], ms(q[i]))) for i in n]
    msv = [ms(x) for x in get("v")]
    w0 = [b16(_dot(a_ak[i], msv[i])) for i in n]
    yield
    at = get("at")
    x1 = [_dot(t_inv[i], _cat1(ms(at[i]), ms(w0[i]))) for i in n]
    ap = [b16(x[:, :LANES]) for x in x1]
    u0 = [b16(x[:, LANES:]) for x in x1]
    yield
    zero = jnp.zeros((2 * c, LANES), BF16)
    ry = [_dot(a_rbk[i], _cat0(_cat1(ms(ap[i]), ms(u0[i])), _cat1(zero, msv[i]))) for i in n]
    yield
    bh, kh, v, rt32, pc = [get(name) for name in ("bh", "kh", "v", "rt", "pc")]
    mh = [_dot(_cat0(bh[i], kh[i]), _cat0(_cat1(ap[i], u0[i]), _cat1(zero[:c], v[i])), _TN) for i in n]
    rp = [b16(rt32[i] + ry[i][:, :LANES]) for i in n]
    yield
    blocks = lambda x: jnp.where(pair_masks[0], x[:c], x[c:])
    m = [b16(blocks(mh[i][:, :LANES]) + jnp.where(eye_mask, pc[i], 0.0)) for i in n]
    hadd = [blocks(mh[i][:, LANES:]) for i in n]

    def apply_state(k):
        for j in range(n_pairs):
            i = k * n_pairs + j
            yh = _dot(_cat0(rp[i], m[i]), ms(b16(hs[j])))
            emit(chunk_ids[k], j, yh[:c] + ry[i][:, LANES:])
            hs[j] = yh[c:] + hadd[i]

    return [functools.partial(apply_state, k) for k in range(len(chunk_ids))]


def _drive(stages, fillers, per_stage):
    while True:
        try:
            next(stages)
        except StopIteration as done:
            return done.value
        for _ in range(per_stage):
            if fillers:
                fillers.pop(0)()


def _rwkv_body(tl, width, p_ref, w0_ref, wwa_ref, a0_ref, g2_ref, kk_ref, ka_ref, rk_ref, lnw_ref, lnb_ref,
               ones_ref, y_ref, h_ref, pre_ref, twa_ref, g_ref, bonus_ref, ss_ref, rt_ref, ops_ref, pc_ref,
               yacc_ref):
    @pl.when(pl.program_id(1) == 0)
    def _():
        h_ref[...] = jnp.zeros_like(h_ref)

    c = CHUNK
    w3 = 3 * width
    n_pairs = width // LANES
    r_cols, k_cols, v_cols = slice(0, width), slice(width, 2 * width), slice(2 * width, w3)
    lane = lax.broadcasted_iota(jnp.int32, (1, LANES), 1)
    m0 = lane < HEAD
    ones = ones_ref[...]
    nk = ones.shape[0]
    segsum = lambda x: _cat1(*[_mm(x[:, i:i + nk], ones) for i in range(0, width, nk)])
    rowc = lax.broadcasted_iota(jnp.int32, (c, LANES), 0)
    colc = lax.broadcasted_iota(jnp.int32, (c, LANES), 1) % HEAD
    masks = ([m0, jnp.logical_not(m0)], colc < rowc, colc <= rowc, colc == rowc)
    ltri = (lax.broadcasted_iota(jnp.int32, (c, c), 1) <= lax.broadcasted_iota(jnp.int32, (c, c), 0)).astype(BF16)
    ltri2 = _cat1(ltri, ltri)

    def low_rank(rows):
        wa = p_ref[0, rows, w3:w3 + LANES]
        twa_ref[rows, :] = _mm(jnp.where(m0, jnp.tanh(wa), wa), wwa_ref[...])
        g_ref[rows, :] = _mm(jax.nn.sigmoid(p_ref[0, rows, w3 + LANES:]), g2_ref[...])

    def elementwise(rows):
        pre_ref[3, rows, :] = -math.exp(-0.5) * jax.nn.sigmoid(w0_ref[...] + twa_ref[rows, :width])
        a = jax.nn.sigmoid(a0_ref[...] + twa_ref[rows, width:])
        k = p_ref[0, rows, k_cols]
        k2 = k * (1.0 + (a - 1.0) * ka_ref[...])
        pre_ref[0, rows, :] = k2
        pre_ref[1, rows, :] = k * kk_ref[...]
        pre_ref[2, rows, :] = a
        bonus_ref[rows, :] = p_ref[0, rows, r_cols] * k2 * rk_ref[...]

    def head_sums(rows):
        kk = pre_ref[1, rows, :]
        ss_ref[rows, :] = segsum(kk * kk)
        bonus_ref[rows, :] = segsum(bonus_ref[rows, :])

    def normalise(rows):
        kk = pre_ref[1, rows, :] * lax.rsqrt(jnp.maximum(ss_ref[rows, :], 1e-24))
        pre_ref[2, rows, :] = kk * pre_ref[2, rows, :]
        pre_ref[1, rows, :] = -kk
        bonus_ref[rows, :] = bonus_ref[rows, :] * p_ref[0, rows, v_cols]

    def decay_scale(ci):
        rows = slice(ci * c, (ci + 1) * c)
        lwc = pre_ref[3, rows, :]
        cum = _dot(ltri2, _cat0(*_split(lwc)))
        pw = jnp.exp(cum)
        pinv = jnp.exp(-cum)
        kt = pre_ref[0, rows, :] * pinv
        bt = pre_ref[2, rows, :] * pinv
        pc = pw[c - 1:c, :]
        rt_ref[rows, :] = p_ref[0, rows, r_cols] * pw
        for i, x in enumerate((pre_ref[1, rows, :] * jnp.exp(cum - lwc), kt, bt, bt * pc, kt * pc,
                               p_ref[0, rows, v_cols])):
            ops_ref[i, rows, :] = x.astype(BF16)
        pc_ref[ci:ci + 1, :] = pc

    op_names = ("at", "kt", "bt", "bh", "kh", "v")

    def fetch(name, ci, j):
        rows, lanes = slice(ci * c, (ci + 1) * c), slice(j * LANES, (j + 1) * LANES)
        if name == "rt":
            return rt_ref[rows, lanes]
        if name == "pc":
            return pc_ref[ci:ci + 1, lanes]
        return ops_ref[op_names.index(name), rows, lanes]

    def centre(rows):
        y = yacc_ref[rows, :]
        yacc_ref[rows, :] = y - segsum(y) * (1.0 / HEAD)

    def group_norm(rows):
        yc = yacc_ref[rows, :]
        var = segsum(yc * yc) * (1.0 / HEAD)
        yn = yc * lax.rsqrt(var + LNX_EPS) * lnw_ref[...] + lnb_ref[...]
        y_ref[0, rows, :] = ((yn + bonus_ref[rows, :]) * g_ref[rows, :]).astype(BF16)

    gc = WKV_GROUP
    n_groups = tl // (c * gc)
    half = c * gc // 2

    def prepare(gi):
        r0 = gi * gc * c
        blocks = [slice(r0 + i * c, r0 + (i + 1) * c) for i in range(gc)]
        halves = [slice(r0, r0 + half), slice(r0 + half, r0 + 2 * half)]
        return ([functools.partial(low_rank, h) for h in halves] + [functools.partial(elementwise, b) for b in blocks]
                + [functools.partial(head_sums, h) for h in halves] + [functools.partial(normalise, b) for b in blocks]
                + [functools.partial(decay_scale, ci) for ci in range(gi * gc, (gi + 1) * gc)])

    def finish(gi):
        r0 = gi * gc * c
        halves = [slice(r0, r0 + half), slice(r0 + half, r0 + 2 * half)]
        return [functools.partial(f, h) for f in (centre, group_norm) for h in halves]

    def emit(ci, j, y):
        yacc_ref[ci * c:(ci + 1) * c, j * LANES:(j + 1) * LANES] = y

    for thunk in prepare(0):
        thunk()
    hs = [h_ref[j] for j in range(n_pairs)]
    state_steps = []
    for gi in range(n_groups):
        prep = prepare(gi + 1) if gi + 1 < n_groups else []
        fillers = []
        for step in state_steps:
            fillers += [step] + prep[:2]
            prep = prep[2:]
        fillers += prep + (finish(gi - 1) if gi > 0 else [])
        chunk_ids = list(range(gi * gc, (gi + 1) * gc))
        state_steps = _drive(_wkv_stages(fetch, chunk_ids, hs, emit, masks), fillers, FILL_PER_STAGE)
        for thunk in fillers:
            thunk()
    for thunk in state_steps + finish(n_groups - 1):
        thunk()
    for j in range(n_pairs):
        h_ref[j] = hs[j]


def _rwkv(p3, w0, wwa, a0, g2, k_k, k_a, r_k, lnx_w, lnx_b, ones_bd, width, tl):
    b, l, n_rwkv = p3.shape
    assert l % tl == 0 and tl % (CHUNK * WKV_GROUP) == 0 and (tl // CHUNK) % SUBLANES == 0
    n_pairs = width // LANES
    consts = [w0, wwa, a0, g2, k_k, k_a, r_k, lnx_w, lnx_b, ones_bd]
    rows = lambda n=width: pltpu.VMEM((tl, n), F32)
    return pl.pallas_call(
        functools.partial(_rwkv_body, tl, width),
        grid=(b, l // tl),
        in_specs=[pl.BlockSpec((1, tl, n_rwkv), lambda i, j: (i, j, 0))] + [_const_spec(c.shape) for c in consts],
        out_specs=pl.BlockSpec((1, tl, width), lambda i, j: (i, j, 0)),
        out_shape=jax.ShapeDtypeStruct((b, l, width), BF16),
        scratch_shapes=[pltpu.VMEM((n_pairs, HEAD, LANES), F32), pltpu.VMEM((4, tl, width), F32),
                        rows(2 * width), rows(), rows(), rows(), rows(), pltpu.VMEM((6, tl, width), BF16),
                        pltpu.VMEM((tl // CHUNK, width), F32), rows()],
        compiler_params=_params(("arbitrary", "arbitrary")),
        name="rwkv7",
    )(p3, *consts)


def _s5_disc_body(are_ref, aim_ref, dt_ref, bre_ref, bim_ref, cre_ref, cim_ref,
                  abr_ref, abi_ref, bbr_ref, bbi_ref, bar_ref, bai_ref, car_ref, cai_ref):
    lre, lim, dt = are_ref[...], aim_ref[...], jnp.exp(dt_ref[...])
    mag = jnp.exp(lre * dt)
    abr = mag * jnp.cos(lim * dt)
    abi = mag * jnp.sin(lim * dt)
    inv = 1.0 / (lre * lre + lim * lim)
    nr, ni = abr - 1.0, abi
    cr = (nr * lre + ni * lim) * inv
    ci = (ni * lre - nr * lim) * inv
    bre, bim = bre_ref[...], bim_ref[...]
    bbr = cr * bre - ci * bim
    bbi = cr * bim + ci * bre
    cre, cim = cre_ref[...], cim_ref[...]
    abr_ref[...] = abr
    abi_ref[...] = abi
    bbr_ref[...] = bbr
    bbi_ref[...] = bbi
    bar_ref[...] = bbr * abr - bbi * abi
    bai_ref[...] = bbr * abi + bbi * abr
    car_ref[...] = cre * abr - cim * abi
    cai_ref[...] = cre * abi + cim * abr


def _s5_disc(a_re, a_im, log_step, b_re, b_im, c_re, c_im):
    g, p, c = b_re.shape
    per_channel = lambda x: jnp.broadcast_to(x.reshape(g, 1, -1), (g, c, p)).reshape(g * c, p)
    rows = lambda x: x.transpose(0, 2, 1).reshape(g * c, p)
    outs = pl.pallas_call(
        _s5_disc_body,
        out_shape=[jax.ShapeDtypeStruct((g * c, p), F32)] * 8,
        name="s5_disc",
    )(per_channel(a_re), per_channel(a_im), per_channel(log_step), rows(b_re), rows(b_im),
      c_re.reshape(g * c, p), c_im.reshape(g * c, p))
    abr, abi, bbr, bbi, bar, bai, car, cai = [x.reshape(g, c, p) for x in outs]
    return abr[:, 0], abi[:, 0], (bbr, bbi), (bar, bai), (car, cai)


def _s5_body(tt, n_sub, nh, u_ref, perm_ref, permt_ref, b2_ref, are_ref, aim_ref, c_ref, ca_ref, cb_ref, d_ref,
             wglu_ref, bglu_ref, y_ref, st_ref, s_ref):
    @pl.when(pl.program_id(0) == 0)
    def _():
        s_ref[...] = jnp.zeros_like(s_ref)

    nb = SUBLANES
    rows = nb * tt
    half = rows // 2
    width = u_ref.shape[-1]
    kin = width // nh
    nsh = are_ref.shape[-1]
    re_c, im_c = slice(0, nsh), slice(nsh, 2 * nsh)
    permt = permt_ref[...]
    permt2 = _cat1(permt, permt)
    us = [u_ref[:, i * tt:(i + 1) * tt, :].reshape(rows, width) for i in range(n_sub)]
    u_even = []
    for i in range(n_sub):
        u_tm = jnp.dot(perm_ref[...], us[i].astype(BF16), preferred_element_type=F32).astype(BF16)
        u_even.append(u_tm[:half])
        for h in range(nh):
            cols = slice(h * kin, (h + 1) * kin)
            st_ref[i, h, nb:, :] = jnp.dot(_cat1(u_tm[half:, cols], u_tm[:half, cols]), b2_ref[h],
                                           preferred_element_type=F32)
    for i in range(n_sub):
        for h in range(nh):
            are, aim = are_ref[h], aim_ref[h]
            a2re = jnp.broadcast_to(are * are - aim * aim, (nb, nsh))
            a2im = jnp.broadcast_to(2.0 * are * aim, (nb, nsh))
            st_ref[i, h, :nb, :] = s_ref[h]

            def step(k, carry):
                sre, sim = carry
                r8 = pl.ds(pl.multiple_of((k + 1) * nb, nb), nb)
                nre = a2re * sre - a2im * sim + st_ref[i, h, r8, re_c]
                nim = a2re * sim + a2im * sre + st_ref[i, h, r8, im_c]
                st_ref[i, h, r8, re_c] = nre
                st_ref[i, h, r8, im_c] = nim
                return nre, nim

            sre, sim = lax.fori_loop(0, tt // 2, step, (s_ref[h, :, re_c], s_ref[h, :, im_c]), unroll=True)
            s_ref[h, :, re_c] = sre
            s_ref[h, :, im_c] = sim
        y_odd = _cat1(*[jnp.dot(st_ref[i, h, nb:, :].astype(BF16), c_ref[h], preferred_element_type=F32)
                        for h in range(nh)])
        y_even = _cat1(*[jnp.dot(st_ref[i, h, :half, :].astype(BF16), ca_ref[h], preferred_element_type=F32)
                         + jnp.dot(u_even[i][:, h * kin:(h + 1) * kin], cb_ref[h], preferred_element_type=F32)
                         for h in range(nh)])
        y = _dot(permt2, _cat0(*_split(_cat0(y_even, y_odd))))
        y = jax.nn.gelu(y + d_ref[...] * us[i])
        y = y * jax.nn.sigmoid(jnp.dot(y.astype(BF16), wglu_ref[...], preferred_element_type=F32) + bglu_ref[...])
        y_ref[:, i * tt:(i + 1) * tt, :] = y.astype(BF16).reshape(nb, tt, width)


def _s5(u3, perm, b2_h, are, aim, c_h, ca_h, cb_h, d, w_glu, b_glu, tt, n_sub):
    b, l, width = u3.shape
    assert l % (tt * n_sub) == 0 and b == SUBLANES and tt % 2 == 0
    nh, _, nsh2 = b2_h.shape
    consts = [perm, perm.T, b2_h, are, aim, c_h, ca_h, cb_h, d, w_glu, b_glu]
    blk = pl.BlockSpec((b, tt * n_sub, width), lambda i: (0, i, 0))
    return pl.pallas_call(
        functools.partial(_s5_body, tt, n_sub, nh),
        grid=(l // (tt * n_sub),),
        in_specs=[blk] + [_const_spec(c.shape) for c in consts],
        out_specs=blk,
        out_shape=jax.ShapeDtypeStruct((b, l, width), BF16),
        scratch_shapes=[pltpu.VMEM((n_sub, nh, b + b * tt // 2, nsh2), F32), pltpu.VMEM((nh, b, nsh2), F32)],
        compiler_params=_params(("arbitrary",)),
        name="s5",
    )(u3, *consts)


def _mix_ffn_body(tm, d, f, cb, x_ref, yr_ref, ys_ref, gate_ref, wbr_ref, wbs_ref, wout_ref, gmix_ref,
                  gpre_ref, wup_ref, cw_ref, cbias_ref, wdown_ref, gpost_ref, o_ref, carry_ref, act_ref):
    @pl.when(pl.program_id(1) == 0)
    def _():
        carry_ref[...] = jnp.zeros_like(carry_ref)

    blocks = [slice(i * tm // ROW_SPLIT, (i + 1) * tm // ROW_SPLIT) for i in range(ROW_SPLIT)]
    mixed = [gate_ref[0, r, :d] * jnp.dot(yr_ref[0, r], wbr_ref[...], preferred_element_type=F32)
             + gate_ref[0, r, d:] * jnp.dot(ys_ref[0, r], wbs_ref[...], preferred_element_type=F32) for r in blocks]
    mixed = [jnp.dot(m.astype(BF16), wout_ref[...], preferred_element_type=F32) for m in mixed]
    xs = [x_ref[0, r] + _rms(m, gmix_ref[...]) for r, m in zip(blocks, mixed)]

    h = _cat0(*[_rms(x, gpre_ref[...]).astype(BF16) for x in xs])
    nb = SUBLANES
    sub = lax.broadcasted_iota(jnp.int32, (nb, 1), 0)

    def conv(cols):
        z = jnp.dot(h, wup_ref[:, cols], preferred_element_type=F32)
        tail = carry_ref[:, cols]
        carry_ref[:, cols] = z[tm - nb:, :]
        out = cbias_ref[:, cols] + cw_ref[CONV_WIDTH - 1:CONV_WIDTH, cols] * z
        for s in range(1, CONV_WIDTH):
            zr = pltpu.roll(z, s, 0)
            top = jnp.where(sub < s, pltpu.roll(tail, s, 0), zr[:nb])
            zs = jnp.concatenate([top, zr[nb:]], axis=0)
            out = out + cw_ref[CONV_WIDTH - 1 - s:CONV_WIDTH - s, cols] * zs
        return out

    for j in range(f // cb):
        gate = conv(slice(j * cb, (j + 1) * cb))
        val = conv(slice(f + j * cb, f + (j + 1) * cb))
        act_ref[:, j * cb:(j + 1) * cb] = (jax.nn.gelu(gate) * val).astype(BF16)
    ys = [jnp.dot(act_ref[r, :], wdown_ref[...], preferred_element_type=F32) for r in blocks]
    for r, x, y in zip(blocks, xs, ys):
        o_ref[0, r] = x + _rms(y, gpost_ref[...])


def _mix_ffn(x3, yr, ys, gates, wbr, wbs, wout, g_mix, g_pre, w_up, conv_w, conv_b, w_down, g_post, tm):
    b, l, d = x3.shape
    f = w_down.shape[0]
    cb = 2 * LANES
    assert l % tm == 0 and f % cb == 0
    consts = [wbr, wbs, wout, g_mix, g_pre, w_up, conv_w, conv_b, w_down, g_post]
    tile = lambda a: pl.BlockSpec((1, tm, a.shape[-1]), lambda i, j: (i, j, 0))
    acts = [x3, yr, ys, gates]
    return pl.pallas_call(
        functools.partial(_mix_ffn_body, tm, d, f, cb),
        grid=(b, l // tm),
        in_specs=[tile(a) for a in acts] + [_const_spec(c.shape) for c in consts],
        out_specs=tile(x3),
        out_shape=jax.ShapeDtypeStruct((b, l, d), F32),
        scratch_shapes=[pltpu.VMEM((SUBLANES, 2 * f), F32), pltpu.VMEM((tm, f), BF16)],
        compiler_params=_params(("arbitrary", "arbitrary")),
        name="mix_ffn",
    )(*acts, *consts)


def _block_diag(blocks):
    g, r, c = blocks.shape
    eye = jnp.eye(g, dtype=blocks.dtype)
    return (blocks[:, :, None, :] * eye[:, None, :, None]).reshape(g * r, g * c)


def _layer(x, norm_mix_pre, norm_mix_post, norm_ffn_pre, norm_ffn_post, w_in, b_gate, mu, w0, w2, a0, a2, g2,
           k_k, k_a, r_k, lnx_w, lnx_b, s5_a_re, s5_a_im, s5_b_re, s5_b_im, s5_c_re, s5_c_im, s5_d, s5_log_step,
           s5_w_glu, s5_b_glu, w_branch_rwkv, w_branch_s5, w_out, ffn_w_up, ffn_conv_w, ffn_conv_b, ffn_w_down,
           tiles):
    b, l, d = x.shape
    width = w0.shape[0]
    s5_width = s5_d.shape[0]
    n_rwkv = mu.shape[0]
    rank_w, rank_a = w2.shape[0], a2.shape[0]
    assert rank_w == HEAD and rank_a == HEAD and g2.shape[0] == LANES and b == SUBLANES
    rowv = lambda vec: vec.reshape(1, -1)
    p, u, gates = _in_proj(x, rowv(norm_mix_pre), w_in, rowv(b_gate), rowv(mu), n_rwkv, s5_width,
                           tiles["in"])

    wwa = jnp.zeros((LANES, 2 * width), F32).at[:rank_w, :width].set(w2).at[rank_w:, width:].set(a2)
    ones_bd = _block_diag(jnp.ones((MXU_K // HEAD, HEAD, HEAD), F32)).astype(BF16)
    y_rwkv = _rwkv(p, rowv(w0), wwa.astype(BF16), rowv(a0), g2.astype(BF16), rowv(k_k), rowv(k_a), rowv(r_k),
                   rowv(lnx_w), rowv(lnx_b), ones_bd, width, tiles["rwkv"])

    abr, abi, bb, ba, ca = _s5_disc(s5_a_re, s5_a_im, s5_log_step, s5_b_re, s5_b_im, s5_c_re, s5_c_im)
    n_grp = s5_a_re.shape[0]
    gh = LANES // S5_GROUP
    nh = n_grp // gh
    parts = lambda a: a.reshape((nh, gh) + a.shape[1:])
    bd = jax.vmap(_block_diag)
    tr = lambda a: parts(a).transpose(0, 1, 3, 2)
    re_im = lambda z: jnp.concatenate([bd(parts(z[0])), bd(parts(z[1]))], axis=2)
    b2_h = jnp.concatenate([re_im(bb), re_im(ba)], axis=1)
    c_h = jnp.concatenate([bd(tr(s5_c_re)), -bd(tr(s5_c_im))], axis=1)
    ca_h = jnp.concatenate([bd(tr(ca[0])), -bd(tr(ca[1]))], axis=1)
    cb = jnp.einsum("gdp,gcp->gcd", s5_c_re, bb[0]) - jnp.einsum("gdp,gcp->gcd", s5_c_im, bb[1])
    cb_h = bd(parts(cb))
    tt = tiles["s5"]
    ridx = jnp.arange(b * tt)
    step = 2 * ((ridx % (b * tt // 2)) // b) + ridx // (b * tt // 2)
    perm = (ridx[None, :] == ((ridx % b) * tt + step)[:, None]).astype(BF16)
    y_s5 = _s5(u, perm, b2_h.astype(BF16), abr.reshape(nh, 1, -1), abi.reshape(nh, 1, -1), c_h.astype(BF16),
               ca_h.astype(BF16), cb_h.astype(BF16), rowv(s5_d), s5_w_glu.astype(BF16), rowv(s5_b_glu), tt,
               tiles["s5_sub"])

    return _mix_ffn(x, y_rwkv, y_s5, gates, w_branch_rwkv.astype(BF16),
                    w_branch_s5.astype(BF16), w_out.astype(BF16), rowv(norm_mix_post), rowv(norm_ffn_pre),
                    ffn_w_up.astype(BF16), ffn_conv_w, rowv(ffn_conv_b), ffn_w_down.astype(BF16),
                    rowv(norm_ffn_post), tiles["ffn"])


TILES = {"in": 512, "rwkv": 1024, "s5": 32, "s5_sub": 4, "ffn": 512}


def kernel(x, norm_mix_pre, norm_mix_post, norm_ffn_pre, norm_ffn_post, w_in, b_gate, rwkv_shift_mu, rwkv_w0, rwkv_w2, rwkv_a0, rwkv_a2, rwkv_g2, rwkv_k_k, rwkv_k_a, rwkv_r_k, rwkv_lnx_w, rwkv_lnx_b, s5_a_re, s5_a_im, s5_b_re, s5_b_im, s5_c_re, s5_c_im, s5_d, s5_log_step, s5_w_glu, s5_b_glu, w_branch_rwkv, w_branch_s5, w_out, ffn_w_up, ffn_conv_w, ffn_conv_b, ffn_w_down):
    depth = w_in.shape[0]
    for i in range(depth):
        x = _layer(x, norm_mix_pre[i], norm_mix_post[i], norm_ffn_pre[i], norm_ffn_post[i], w_in[i], b_gate[i],
                   rwkv_shift_mu[i], rwkv_w0[i], rwkv_w2[i], rwkv_a0[i], rwkv_a2[i], rwkv_g2[i], rwkv_k_k[i],
                   rwkv_k_a[i], rwkv_r_k[i].reshape(-1), rwkv_lnx_w[i], rwkv_lnx_b[i], s5_a_re[i], s5_a_im[i],
                   s5_b_re[i], s5_b_im[i], s5_c_re[i], s5_c_im[i], s5_d[i], s5_log_step[i], s5_w_glu[i],
                   s5_b_glu[i], w_branch_rwkv[i], w_branch_s5[i], w_out[i], ffn_w_up[i], ffn_conv_w[i],
                   ffn_conv_b[i], ffn_w_down[i], TILES)
    return x
```

```python
import functools
import math

import jax
import jax.numpy as jnp
from jax import lax
from jax.experimental import pallas as pl
from jax.experimental.pallas import tpu as pltpu

F32 = jnp.float32
BF16 = jnp.bfloat16

NORM_EPS = 1e-6
LNX_EPS = 64e-5
HEAD = 64
CHUNK = 64
WKV_GROUP = 4
FILL_PER_STAGE = 3
LANES = 128
SUBLANES = 8
S5_GROUP = 16
S5_STATE = 64
CONV_WIDTH = 3
ROW_SPLIT = 2
MXU_K = 256
VMEM_LIMIT = 56 * 1024 * 1024


def _rms(x, g):
    return x * lax.rsqrt(jnp.mean(x * x, axis=-1, keepdims=True) + NORM_EPS) * g


def _split(x):
    hi = x.astype(BF16)
    lo = (x - hi.astype(F32)).astype(BF16)
    return hi, lo


_NN = (((1,), (0,)), ((), ()))
_NT = (((1,), (1,)), ((), ()))
_TN = (((0,), (0,)), ((), ()))


def _dot(a, b, dims=_NN):
    return lax.dot_general(a, b, dims, preferred_element_type=F32)


def _mm(a, b):
    return _dot(a.astype(BF16), b.astype(BF16))


def _cat0(*xs):
    return jnp.concatenate(xs, axis=0)


def _cat1(*xs):
    return jnp.concatenate(xs, axis=1)


def _const_spec(shape):
    nd = len(shape)
    return pl.BlockSpec(shape, lambda *_: (0,) * nd)


def _params(sem):
    return pltpu.CompilerParams(dimension_semantics=sem, vmem_limit_bytes=VMEM_LIMIT)


def _in_proj_body(tm, n_rwkv, n_s5, x_ref, g_ref, w32_ref, bg_ref, mu_ref, p_ref, u_ref, gate_ref, carry_ref,
                  w_ref):
    @pl.when((pl.program_id(0) == 0) & (pl.program_id(1) == 0))
    def _():
        for c0 in range(0, w_ref.shape[1], MXU_K):
            w_ref[:, c0:c0 + MXU_K] = w32_ref[:, c0:c0 + MXU_K].astype(BF16)

    @pl.when(pl.program_id(1) == 0)
    def _():
        carry_ref[...] = jnp.zeros_like(carry_ref)

    blocks = [slice(i * tm // ROW_SPLIT, (i + 1) * tm // ROW_SPLIT) for i in range(ROW_SPLIT)]
    ps = []
    for r in blocks:
        h = _rms(x_ref[0, r], g_ref[...]).astype(BF16)
        ps.append(jnp.dot(h, w_ref[:, :n_rwkv], preferred_element_type=F32))
        u_ref[0, r] = jnp.dot(h, w_ref[:, n_rwkv:n_rwkv + n_s5], preferred_element_type=F32)
        gate_ref[0, r] = jax.nn.sigmoid(
            jnp.dot(h, w_ref[:, n_rwkv + n_s5:], preferred_element_type=F32) + bg_ref[...]).astype(BF16)
    p = _cat0(*ps)
    row = lax.broadcasted_iota(jnp.int32, (tm, 1), 0)
    prev = jnp.where(row == 0, carry_ref[...], pltpu.roll(p, 1, 0))
    carry_ref[...] = p[tm - 1:tm, :]
    p_ref[0] = p + (prev - p) * mu_ref[...]


def _in_proj(x3, g, w_in, b_gate, mu, n_rwkv, n_s5, tm):
    b, l, d = x3.shape
    assert l % tm == 0
    n_in = w_in.shape[1]
    n_gate = n_in - n_rwkv - n_s5
    tile = lambda n: pl.BlockSpec((1, tm, n), lambda i, j: (i, j, 0))
    return pl.pallas_call(
        functools.partial(_in_proj_body, tm, n_rwkv, n_s5),
        grid=(b, l // tm),
        in_specs=[tile(d), _const_spec((1, d)), _const_spec((d, n_in)), _const_spec((1, n_gate)),
                  _const_spec((1, n_rwkv))],
        out_specs=[tile(n_rwkv), tile(n_s5), tile(n_gate)],
        out_shape=[jax.ShapeDtypeStruct((b, l, n_rwkv), F32), jax.ShapeDtypeStruct((b, l, n_s5), F32),
                   jax.ShapeDtypeStruct((b, l, n_gate), BF16)],
        scratch_shapes=[pltpu.VMEM((1, n_rwkv), F32), pltpu.VMEM((d, n_in), BF16)],
        compiler_params=_params(("arbitrary", "arbitrary")),
        name="in_proj",
    )(x3, g, w_in, b_gate, mu)


def _mstack(x, head_masks):
    z = jnp.zeros_like(x)
    return jnp.concatenate([jnp.where(m, x, z) for m in head_masks], axis=0)


def _wkv_stages(fetch, chunk_ids, hs, emit, masks):
    pair_masks, strict, incl, eye_mask = masks
    eye = eye_mask.astype(F32)
    c = CHUNK
    ms = lambda x: _mstack(x, pair_masks)
    b16 = lambda x: x.astype(BF16)
    n_pairs = len(hs)
    ids = [(k, j) for k in range(len(chunk_ids)) for j in range(n_pairs)]
    get = lambda name: [fetch(name, chunk_ids[k], j) for k, j in ids]
    n = range(len(ids))
    at, rt, bt, kt = get("at"), [b16(x) for x in get("rt")], get("bt"), get("kt")
    g = [_dot(_cat0(at[i], rt[i]), _cat0(ms(bt[i]), ms(kt[i])), _NT) for i in n]
    yield
    a_ab = [jnp.where(strict, x[:c, :LANES], 0.0) for x in g]
    a_ak = [b16(jnp.where(strict, x[:c, LANES:], 0.0)) for x in g]
    a_rbk = [b16(_cat1(jnp.where(incl, x[c:, :LANES], 0.0), jnp.where(incl, x[c:, LANES:], 0.0))) for x in g]
    ab16 = [b16(x) for x in a_ab]
    q = [b16(_dot(ab16[i], ms(ab16[i]))) for i in n]
    t_inv = [eye + x for x in a_ab]
    yield
    n_sq = int(math.log2(c)) - 1
    for s in range(n_sq - 1):
        tq = [_dot(_cat0(b16(t_inv[i]), q[i]), ms(q[i])) for i in n]
        t_inv = [t_inv[i] + tq[i][:c] for i in n]
        q = [b16(tq[i][c:]) for i in n]
        yield
    t16 = [b16(t_inv[i]) for i in n]
    t_inv = [b16(t_inv[i] + _dot(t16[i], ms(q[i]))) for i in n]
    msv = [ms(x) for x in get("v")]
    w0 = [b16(_dot(a_ak[i], msv[i])) for i in n]
    yield
    at = get("at")
    x1 = [_dot(t_inv[i], _cat1(ms(at[i]), ms(w0[i]))) for i in n]
    ap = [b16(x[:, :LANES]) for x in x1]
    u0 = [b16(x[:, LANES:]) for x in x1]
    yield
    zero = jnp.zeros((2 * c, LANES), BF16)
    ry = [_dot(a_rbk[i], _cat0(_cat1(ms(ap[i]), ms(u0[i])), _cat1(zero, msv[i]))) for i in n]
    yield
    bh, kh, v, rt32, pc = [get(name) for name in ("bh", "kh", "v", "rt", "pc")]
    mh = [_dot(_cat0(bh[i], kh[i]), _cat0(_cat1(ap[i], u0[i]), _cat1(zero[:c], v[i])), _TN) for i in n]
    rp = [b16(rt32[i] + ry[i][:, :LANES]) for i in n]
    yield
    blocks = lambda x: jnp.where(pair_masks[0], x[:c], x[c:])
    m = [b16(blocks(mh[i][:, :LANES]) + jnp.where(eye_mask, pc[i], 0.0)) for i in n]
    hadd = [blocks(mh[i][:, LANES:]) for i in n]

    def apply_state(k):
        for j in range(n_pairs):
            i = k * n_pairs + j
            yh = _dot(_cat0(rp[i], m[i]), ms(b16(hs[j])))
            emit(chunk_ids[k], j, yh[:c] + ry[i][:, LANES:])
            hs[j] = yh[c:] + hadd[i]

    return [functools.partial(apply_state, k) for k in range(len(chunk_ids))]


def _drive(stages, fillers, per_stage):
    while True:
        try:
            next(stages)
        except StopIteration as done:
            return done.value
        for _ in range(per_stage):
            if fillers:
                fillers.pop(0)()


def _rwkv_body(tl, width, p_ref, w0_ref, wwa_ref, a0_ref, g2_ref, kk_ref, ka_ref, rk_ref, lnw_ref, lnb_ref,
               ones_ref, y_ref, h_ref, pre_ref, twa_ref, g_ref, bonus_ref, ss_ref, rt_ref, ops_ref, pc_ref,
               yacc_ref):
    @pl.when(pl.program_id(1) == 0)
    def _():
        h_ref[...] = jnp.zeros_like(h_ref)

    c = CHUNK
    w3 = 3 * width
    n_pairs = width // LANES
    r_cols, k_cols, v_cols = slice(0, width), slice(width, 2 * width), slice(2 * width, w3)
    lane = lax.broadcasted_iota(jnp.int32, (1, LANES), 1)
    m0 = lane < HEAD
    ones = ones_ref[...]
    nk = ones.shape[0]
    segsum = lambda x: _cat1(*[_mm(x[:, i:i + nk], ones) for i in range(0, width, nk)])
    rowc = lax.broadcasted_iota(jnp.int32, (c, LANES), 0)
    colc = lax.broadcasted_iota(jnp.int32, (c, LANES), 1) % HEAD
    masks = ([m0, jnp.logical_not(m0)], colc < rowc, colc <= rowc, colc == rowc)
    ltri = (lax.broadcasted_iota(jnp.int32, (c, c), 1) <= lax.broadcasted_iota(jnp.int32, (c, c), 0)).astype(BF16)
    ltri2 = _cat1(ltri, ltri)

    def low_rank(rows):
        wa = p_ref[0, rows, w3:w3 + LANES]
        twa_ref[rows, :] = _mm(jnp.where(m0, jnp.tanh(wa), wa), wwa_ref[...])
        g_ref[rows, :] = _mm(jax.nn.sigmoid(p_ref[0, rows, w3 + LANES:]), g2_ref[...])

    def elementwise(rows):
        pre_ref[3, rows, :] = -math.exp(-0.5) * jax.nn.sigmoid(w0_ref[...] + twa_ref[rows, :width])
        a = jax.nn.sigmoid(a0_ref[...] + twa_ref[rows, width:])
        k = p_ref[0, rows, k_cols]
        k2 = k * (1.0 + (a - 1.0) * ka_ref[...])
        pre_ref[0, rows, :] = k2
        pre_ref[1, rows, :] = k * kk_ref[...]
        pre_ref[2, rows, :] = a
        bonus_ref[rows, :] = p_ref[0, rows, r_cols] * k2 * rk_ref[...]

    def head_sums(rows):
        kk = pre_ref[1, rows, :]
        ss_ref[rows, :] = segsum(kk * kk)
        bonus_ref[rows, :] = segsum(bonus_ref[rows, :])

    def normalise(rows):
        kk = pre_ref[1, rows, :] * lax.rsqrt(jnp.maximum(ss_ref[rows, :], 1e-24))
        pre_ref[2, rows, :] = kk * pre_ref[2, rows, :]
        pre_ref[1, rows, :] = -kk
        bonus_ref[rows, :] = bonus_ref[rows, :] * p_ref[0, rows, v_cols]

    def decay_scale(ci):
        rows = slice(ci * c, (ci + 1) * c)
        lwc = pre_ref[3, rows, :]
        cum = _dot(ltri2, _cat0(*_split(lwc)))
        pw = jnp.exp(cum)
        pinv = jnp.exp(-cum)
        kt = pre_ref[0, rows, :] * pinv
        bt = pre_ref[2, rows, :] * pinv
        pc = pw[c - 1:c, :]
        rt_ref[rows, :] = p_ref[0, rows, r_cols] * pw
        for i, x in enumerate((pre_ref[1, rows, :] * jnp.exp(cum - lwc), kt, bt, bt * pc, kt * pc,
                               p_ref[0, rows, v_cols])):
            ops_ref[i, rows, :] = x.astype(BF16)
        pc_ref[ci:ci + 1, :] = pc

    op_names = ("at", "kt", "bt", "bh", "kh", "v")

    def fetch(name, ci, j):
        rows, lanes = slice(ci * c, (ci + 1) * c), slice(j * LANES, (j + 1) * LANES)
        if name == "rt":
            return rt_ref[rows, lanes]
        if name == "pc":
            return pc_ref[ci:ci + 1, lanes]
        return ops_ref[op_names.index(name), rows, lanes]

    def centre(rows):
        y = yacc_ref[rows, :]
        yacc_ref[rows, :] = y - segsum(y) * (1.0 / HEAD)

    def group_norm(rows):
        yc = yacc_ref[rows, :]
        var = segsum(yc * yc) * (1.0 / HEAD)
        yn = yc * lax.rsqrt(var + LNX_EPS) * lnw_ref[...] + lnb_ref[...]
        y_ref[0, rows, :] = ((yn + bonus_ref[rows, :]) * g_ref[rows, :]).astype(BF16)

    gc = WKV_GROUP
    n_groups = tl // (c * gc)
    half = c * gc // 2

    def prepare(gi):
        r0 = gi * gc * c
        blocks = [slice(r0 + i * c, r0 + (i + 1) * c) for i in range(gc)]
        halves = [slice(r0, r0 + half), slice(r0 + half, r0 + 2 * half)]
        return ([functools.partial(low_rank, h) for h in halves] + [functools.partial(elementwise, b) for b in blocks]
                + [functools.partial(head_sums, h) for h in halves] + [functools.partial(normalise, b) for b in blocks]
                + [functools.partial(decay_scale, ci) for ci in range(gi * gc, (gi + 1) * gc)])

    def finish(gi):
        r0 = gi * gc * c
        halves = [slice(r0, r0 + half), slice(r0 + half, r0 + 2 * half)]
        return [functools.partial(f, h) for f in (centre, group_norm) for h in halves]

    def emit(ci, j, y):
        yacc_ref[ci * c:(ci + 1) * c, j * LANES:(j + 1) * LANES] = y

    for thunk in prepare(0):
        thunk()
    hs = [h_ref[j] for j in range(n_pairs)]
    state_steps = []
    for gi in range(n_groups):
        prep = prepare(gi + 1) if gi + 1 < n_groups else []
        fillers = []
        for step in state_steps:
            fillers += [step] + prep[:2]
            prep = prep[2:]
        fillers += prep + (finish(gi - 1) if gi > 0 else [])
        chunk_ids = list(range(gi * gc, (gi + 1) * gc))
        state_steps = _drive(_wkv_stages(fetch, chunk_ids, hs, emit, masks), fillers, FILL_PER_STAGE)
        for thunk in fillers:
            thunk()
    for thunk in state_steps + finish(n_groups - 1):
        thunk()
    for j in range(n_pairs):
        h_ref[j] = hs[j]


def _rwkv(p3, w0, wwa, a0, g2, k_k, k_a, r_k, lnx_w, lnx_b, ones_bd, width, tl):
    b, l, n_rwkv = p3.shape
    assert l % tl == 0 and tl % (CHUNK * WKV_GROUP) == 0 and (tl // CHUNK) % SUBLANES == 0
    n_pairs = width // LANES
    consts = [w0, wwa, a0, g2, k_k, k_a, r_k, lnx_w, lnx_b, ones_bd]
    rows = lambda n=width: pltpu.VMEM((tl, n), F32)
    return pl.pallas_call(
        functools.partial(_rwkv_body, tl, width),
        grid=(b, l // tl),
        in_specs=[pl.BlockSpec((1, tl, n_rwkv), lambda i, j: (i, j, 0))] + [_const_spec(c.shape) for c in consts],
        out_specs=pl.BlockSpec((1, tl, width), lambda i, j: (i, j, 0)),
        out_shape=jax.ShapeDtypeStruct((b, l, width), BF16),
        scratch_shapes=[pltpu.VMEM((n_pairs, HEAD, LANES), F32), pltpu.VMEM((4, tl, width), F32),
                        rows(2 * width), rows(), rows(), rows(), rows(), pltpu.VMEM((6, tl, width), BF16),
                        pltpu.VMEM((tl // CHUNK, width), F32), rows()],
        compiler_params=_params(("arbitrary", "arbitrary")),
        name="rwkv7",
    )(p3, *consts)


def _s5_disc_body(are_ref, aim_ref, dt_ref, bre_ref, bim_ref, cre_ref, cim_ref,
                  abr_ref, abi_ref, bbr_ref, bbi_ref, bar_ref, bai_ref, car_ref, cai_ref):
    lre, lim, dt = are_ref[...], aim_ref[...], jnp.exp(dt_ref[...])
    mag = jnp.exp(lre * dt)
    abr = mag * jnp.cos(lim * dt)
    abi = mag * jnp.sin(lim * dt)
    inv = 1.0 / (lre * lre + lim * lim)
    nr, ni = abr - 1.0, abi
    cr = (nr * lre + ni * lim) * inv
    ci = (ni * lre - nr * lim) * inv
    bre, bim = bre_ref[...], bim_ref[...]
    bbr = cr * bre - ci * bim
    bbi = cr * bim + ci * bre
    cre, cim = cre_ref[...], cim_ref[...]
    abr_ref[...] = abr
    abi_ref[...] = abi
    bbr_ref[...] = bbr
    bbi_ref[...] = bbi
    bar_ref[...] = bbr * abr - bbi * abi
    bai_ref[...] = bbr * abi + bbi * abr
    car_ref[...] = cre * abr - cim * abi
    cai_ref[...] = cre * abi + cim * abr


def _s5_disc(a_re, a_im, log_step, b_re, b_im, c_re, c_im):
    g, p, c = b_re.shape
    per_channel = lambda x: jnp.broadcast_to(x.reshape(g, 1, -1), (g, c, p)).reshape(g * c, p)
    rows = lambda x: x.transpose(0, 2, 1).reshape(g * c, p)
    outs = pl.pallas_call(
        _s5_disc_body,
        out_shape=[jax.ShapeDtypeStruct((g * c, p), F32)] * 8,
        name="s5_disc",
    )(per_channel(a_re), per_channel(a_im), per_channel(log_step), rows(b_re), rows(b_im),
      c_re.reshape(g * c, p), c_im.reshape(g * c, p))
    abr, abi, bbr, bbi, bar, bai, car, cai = [x.reshape(g, c, p) for x in outs]
    return abr[:, 0], abi[:, 0], (bbr, bbi), (bar, bai), (car, cai)


def _s5_body(tt, n_sub, nh, n_cast, u_ref, perm_ref, permt_ref, b2_ref, are_ref, aim_ref, c_ref, ca_ref, cb_ref,
             d_ref, wglu_ref, bglu_ref, *rest):
    w32_refs, y_ref, w16_refs = rest[:n_cast], rest[n_cast], rest[n_cast + 1:2 * n_cast + 1]
    st_ref, s_ref = rest[2 * n_cast + 1:]
    for w32_ref, w16_ref in zip(w32_refs, w16_refs):
        w16_ref[...] = w32_ref[...].astype(BF16)

    @pl.when(pl.program_id(0) == 0)
    def _():
        s_ref[...] = jnp.zeros_like(s_ref)

    nb = SUBLANES
    rows = nb * tt
    half = rows // 2
    width = u_ref.shape[-1]
    kin = width // nh
    nsh = are_ref.shape[-1]
    re_c, im_c = slice(0, nsh), slice(nsh, 2 * nsh)
    permt = permt_ref[...]
    permt2 = _cat1(permt, permt)
    us = [u_ref[:, i * tt:(i + 1) * tt, :].reshape(rows, width) for i in range(n_sub)]
    u_even = []
    for i in range(n_sub):
        u_tm = jnp.dot(perm_ref[...], us[i].astype(BF16), preferred_element_type=F32).astype(BF16)
        u_even.append(u_tm[:half])
        for h in range(nh):
            cols = slice(h * kin, (h + 1) * kin)
            st_ref[i, h, nb:, :] = jnp.dot(_cat1(u_tm[half:, cols], u_tm[:half, cols]), b2_ref[h],
                                           preferred_element_type=F32)
    for i in range(n_sub):
        for h in range(nh):
            are, aim = are_ref[h], aim_ref[h]
            a2re = jnp.broadcast_to(are * are - aim * aim, (nb, nsh))
            a2im = jnp.broadcast_to(2.0 * are * aim, (nb, nsh))
            st_ref[i, h, :nb, :] = s_ref[h]

            def step(k, carry):
                sre, sim = carry
                r8 = pl.ds(pl.multiple_of((k + 1) * nb, nb), nb)
                nre = a2re * sre - a2im * sim + st_ref[i, h, r8, re_c]
                nim = a2re * sim + a2im * sre + st_ref[i, h, r8, im_c]
                st_ref[i, h, r8, re_c] = nre
                st_ref[i, h, r8, im_c] = nim
                return nre, nim

            sre, sim = lax.fori_loop(0, tt // 2, step, (s_ref[h, :, re_c], s_ref[h, :, im_c]), unroll=True)
            s_ref[h, :, re_c] = sre
            s_ref[h, :, im_c] = sim
        y_odd = _cat1(*[jnp.dot(st_ref[i, h, nb:, :].astype(BF16), c_ref[h], preferred_element_type=F32)
                        for h in range(nh)])
        y_even = _cat1(*[jnp.dot(st_ref[i, h, :half, :].astype(BF16), ca_ref[h], preferred_element_type=F32)
                         + jnp.dot(u_even[i][:, h * kin:(h + 1) * kin], cb_ref[h], preferred_element_type=F32)
                         for h in range(nh)])
        y = _dot(permt2, _cat0(*_split(_cat0(y_even, y_odd))))
        y = jax.nn.gelu(y + d_ref[...] * us[i])
        y = y * jax.nn.sigmoid(jnp.dot(y.astype(BF16), wglu_ref[...], preferred_element_type=F32) + bglu_ref[...])
        y_ref[:, i * tt:(i + 1) * tt, :] = y.astype(BF16).reshape(nb, tt, width)


def _row_blocks(n_rows, n_steps):
    repeat = 1
    while n_rows % (n_steps // repeat) or (n_rows // (n_steps // repeat)) % (2 * SUBLANES):
        repeat *= 2
        assert repeat <= n_steps and n_steps % repeat == 0
    return n_rows // (n_steps // repeat), repeat


def _s5(u3, perm, b2_h, are, aim, c_h, ca_h, cb_h, d, w_glu, b_glu, tt, n_sub, cast):
    b, l, width = u3.shape
    assert l % (tt * n_sub) == 0 and b == SUBLANES and tt % 2 == 0
    nh, _, nsh2 = b2_h.shape
    n_steps = l // (tt * n_sub)
    consts = [perm, perm.T, b2_h, are, aim, c_h, ca_h, cb_h, d, w_glu, b_glu]
    blk = pl.BlockSpec((b, tt * n_sub, width), lambda i: (0, i, 0))

    def cast_spec(w):
        rows, repeat = _row_blocks(w.shape[0], n_steps)
        return pl.BlockSpec((rows, w.shape[1]), lambda i: (i // repeat, 0))

    outs = pl.pallas_call(
        functools.partial(_s5_body, tt, n_sub, nh, len(cast)),
        grid=(n_steps,),
        in_specs=[blk] + [_const_spec(c.shape) for c in consts] + [cast_spec(w) for w in cast],
        out_specs=[blk] + [cast_spec(w) for w in cast],
        out_shape=[jax.ShapeDtypeStruct((b, l, width), BF16)] + [jax.ShapeDtypeStruct(w.shape, BF16) for w in cast],
        scratch_shapes=[pltpu.VMEM((n_sub, nh, b + b * tt // 2, nsh2), F32), pltpu.VMEM((nh, b, nsh2), F32)],
        compiler_params=_params(("arbitrary",)),
        name="s5",
    )(u3, *consts, *cast)
    return outs[0], outs[1:]


def _mix_ffn_body(tm, d, f, cb, x_ref, yr_ref, ys_ref, gate_ref, wbr_ref, wbs_ref, wout_ref, gmix_ref,
                  gpre_ref, wup_ref, cw_ref, cbias_ref, wdown_ref, gpost_ref, o_ref, carry_ref, act_ref):
    @pl.when(pl.program_id(1) == 0)
    def _():
        carry_ref[...] = jnp.zeros_like(carry_ref)

    blocks = [slice(i * tm // ROW_SPLIT, (i + 1) * tm // ROW_SPLIT) for i in range(ROW_SPLIT)]
    mixed = [gate_ref[0, r, :d] * jnp.dot(yr_ref[0, r], wbr_ref[...], preferred_element_type=F32)
             + gate_ref[0, r, d:] * jnp.dot(ys_ref[0, r], wbs_ref[...], preferred_element_type=F32) for r in blocks]
    mixed = [jnp.dot(m.astype(BF16), wout_ref[...], preferred_element_type=F32) for m in mixed]
    xs = [x_ref[0, r] + _rms(m, gmix_ref[...]) for r, m in zip(blocks, mixed)]

    h = _cat0(*[_rms(x, gpre_ref[...]).astype(BF16) for x in xs])
    nb = SUBLANES
    sub = lax.broadcasted_iota(jnp.int32, (nb, 1), 0)

    def conv(cols):
        z = jnp.dot(h, wup_ref[:, cols], preferred_element_type=F32)
        tail = carry_ref[:, cols]
        carry_ref[:, cols] = z[tm - nb:, :]
        out = cbias_ref[:, cols] + cw_ref[CONV_WIDTH - 1:CONV_WIDTH, cols] * z
        for s in range(1, CONV_WIDTH):
            zr = pltpu.roll(z, s, 0)
            top = jnp.where(sub < s, pltpu.roll(tail, s, 0), zr[:nb])
            zs = jnp.concatenate([top, zr[nb:]], axis=0)
            out = out + cw_ref[CONV_WIDTH - 1 - s:CONV_WIDTH - s, cols] * zs
        return out

    for j in range(f // cb):
        gate = conv(slice(j * cb, (j + 1) * cb))
        val = conv(slice(f + j * cb, f + (j + 1) * cb))
        act_ref[:, j * cb:(j + 1) * cb] = (jax.nn.gelu(gate) * val).astype(BF16)
    ys = [jnp.dot(act_ref[r, :], wdown_ref[...], preferred_element_type=F32) for r in blocks]
    for r, x, y in zip(blocks, xs, ys):
        o_ref[0, r] = x + _rms(y, gpost_ref[...])


def _mix_ffn(x3, yr, ys, gates, wbr, wbs, wout, g_mix, g_pre, w_up, conv_w, conv_b, w_down, g_post, tm):
    b, l, d = x3.shape
    f = w_down.shape[0]
    cb = 2 * LANES
    assert l % tm == 0 and f % cb == 0
    consts = [wbr, wbs, wout, g_mix, g_pre, w_up, conv_w, conv_b, w_down, g_post]
    tile = lambda a: pl.BlockSpec((1, tm, a.shape[-1]), lambda i, j: (i, j, 0))
    acts = [x3, yr, ys, gates]
    return pl.pallas_call(
        functools.partial(_mix_ffn_body, tm, d, f, cb),
        grid=(b, l // tm),
        in_specs=[tile(a) for a in acts] + [_const_spec(c.shape) for c in consts],
        out_specs=tile(x3),
        out_shape=jax.ShapeDtypeStruct((b, l, d), F32),
        scratch_shapes=[pltpu.VMEM((SUBLANES, 2 * f), F32), pltpu.VMEM((tm, f), BF16)],
        compiler_params=_params(("arbitrary", "arbitrary")),
        name="mix_ffn",
    )(*acts, *consts)


def _block_diag(blocks):
    g, r, c = blocks.shape
    eye = jnp.eye(g, dtype=blocks.dtype)
    return (blocks[:, :, None, :] * eye[:, None, :, None]).reshape(g * r, g * c)


def _layer(x, norm_mix_pre, norm_mix_post, norm_ffn_pre, norm_ffn_post, w_in, b_gate, mu, w0, w2, a0, a2, g2,
           k_k, k_a, r_k, lnx_w, lnx_b, s5_a_re, s5_a_im, s5_b_re, s5_b_im, s5_c_re, s5_c_im, s5_d, s5_log_step,
           s5_w_glu, s5_b_glu, w_branch_rwkv, w_branch_s5, w_out, ffn_w_up, ffn_conv_w, ffn_conv_b, ffn_w_down,
           tiles):
    b, l, d = x.shape
    width = w0.shape[0]
    s5_width = s5_d.shape[0]
    n_rwkv = mu.shape[0]
    rank_w, rank_a = w2.shape[0], a2.shape[0]
    assert rank_w == HEAD and rank_a == HEAD and g2.shape[0] == LANES and b == SUBLANES
    rowv = lambda vec: vec.reshape(1, -1)
    p, u, gates = _in_proj(x, rowv(norm_mix_pre), w_in, rowv(b_gate), rowv(mu), n_rwkv, s5_width,
                           tiles["in"])

    wwa = jnp.zeros((LANES, 2 * width), F32).at[:rank_w, :width].set(w2).at[rank_w:, width:].set(a2)
    ones_bd = _block_diag(jnp.ones((MXU_K // HEAD, HEAD, HEAD), F32)).astype(BF16)
    y_rwkv = _rwkv(p, rowv(w0), wwa.astype(BF16), rowv(a0), g2.astype(BF16), rowv(k_k), rowv(k_a), rowv(r_k),
                   rowv(lnx_w), rowv(lnx_b), ones_bd, width, tiles["rwkv"])

    abr, abi, bb, ba, ca = _s5_disc(s5_a_re, s5_a_im, s5_log_step, s5_b_re, s5_b_im, s5_c_re, s5_c_im)
    n_grp = s5_a_re.shape[0]
    gh = LANES // S5_GROUP
    nh = n_grp // gh
    parts = lambda a: a.reshape((nh, gh) + a.shape[1:])
    bd = jax.vmap(_block_diag)
    tr = lambda a: parts(a).transpose(0, 1, 3, 2)
    re_im = lambda z: jnp.concatenate([bd(parts(z[0])), bd(parts(z[1]))], axis=2)
    b2_h = jnp.concatenate([re_im(bb), re_im(ba)], axis=1)
    c_h = jnp.concatenate([bd(tr(s5_c_re)), -bd(tr(s5_c_im))], axis=1)
    ca_h = jnp.concatenate([bd(tr(ca[0])), -bd(tr(ca[1]))], axis=1)
    cb = jnp.einsum("gdp,gcp->gcd", s5_c_re, bb[0]) - jnp.einsum("gdp,gcp->gcd", s5_c_im, bb[1])
    cb_h = bd(parts(cb))
    tt = tiles["s5"]
    ridx = jnp.arange(b * tt)
    step = 2 * ((ridx % (b * tt // 2)) // b) + ridx // (b * tt // 2)
    perm = (ridx[None, :] == ((ridx % b) * tt + step)[:, None]).astype(BF16)
    y_s5, (wbr, wbs, wout, wup, wdown) = _s5(
        u, perm, b2_h.astype(BF16), abr.reshape(nh, 1, -1), abi.reshape(nh, 1, -1), c_h.astype(BF16),
        ca_h.astype(BF16), cb_h.astype(BF16), rowv(s5_d), s5_w_glu.astype(BF16), rowv(s5_b_glu), tt,
        tiles["s5_sub"], cast=[w_branch_rwkv, w_branch_s5, w_out, ffn_w_up, ffn_w_down])

    return _mix_ffn(x, y_rwkv, y_s5, gates, wbr, wbs, wout, rowv(norm_mix_post), rowv(norm_ffn_pre), wup, ffn_conv_w,
                    rowv(ffn_conv_b), wdown, rowv(norm_ffn_post), tiles["ffn"])


TILES = {"in": 512, "rwkv": 1024, "s5": 32, "s5_sub": 4, "ffn": 512}


def kernel(x, norm_mix_pre, norm_mix_post, norm_ffn_pre, norm_ffn_post, w_in, b_gate, rwkv_shift_mu, rwkv_w0, rwkv_w2, rwkv_a0, rwkv_a2, rwkv_g2, rwkv_k_k, rwkv_k_a, rwkv_r_k, rwkv_lnx_w, rwkv_lnx_b, s5_a_re, s5_a_im, s5_b_re, s5_b_im, s5_c_re, s5_c_im, s5_d, s5_log_step, s5_w_glu, s5_b_glu, w_branch_rwkv, w_branch_s5, w_out, ffn_w_up, ffn_conv_w, ffn_conv_b, ffn_w_down):
    depth = w_in.shape[0]
    for i in range(depth):
        x = _layer(x, norm_mix_pre[i], norm_mix_post[i], norm_ffn_pre[i], norm_ffn_post[i], w_in[i], b_gate[i],
                   rwkv_shift_mu[i], rwkv_w0[i], rwkv_w2[i], rwkv_a0[i], rwkv_a2[i], rwkv_g2[i], rwkv_k_k[i],
                   rwkv_k_a[i], rwkv_r_k[i].reshape(-1), rwkv_lnx_w[i], rwkv_lnx_b[i], s5_a_re[i], s5_a_im[i],
                   s5_b_re[i], s5_b_im[i], s5_c_re[i], s5_c_im[i], s5_d[i], s5_log_step[i], s5_w_glu[i],
                   s5_b_glu[i], w_branch_rwkv[i], w_branch_s5[i], w_out[i], ffn_w_up[i], ffn_conv_w[i],
                   ffn_conv_b[i], ffn_w_down[i], TILES)
    return x
```

```python
import functools
import math

import jax
import jax.numpy as jnp
from jax import lax
from jax.experimental import pallas as pl
from jax.experimental.pallas import tpu as pltpu

F32 = jnp.float32
BF16 = jnp.bfloat16

NORM_EPS = 1e-6
LNX_EPS = 64e-5
HEAD = 64
CHUNK = 64
WKV_GROUP = 4
FILL_PER_STAGE = 3
LANES = 128
SUBLANES = 8
S5_GROUP = 16
S5_STATE = 64
CONV_WIDTH = 3
ROW_SPLIT = 2
MXU_K = 256
VMEM_LIMIT = 56 * 1024 * 1024


def _rms(x, g):
    return x * lax.rsqrt(jnp.mean(x * x, axis=-1, keepdims=True) + NORM_EPS) * g


def _split(x):
    hi = x.astype(BF16)
    lo = (x - hi.astype(F32)).astype(BF16)
    return hi, lo


_NN = (((1,), (0,)), ((), ()))
_NT = (((1,), (1,)), ((), ()))
_TN = (((0,), (0,)), ((), ()))


def _dot(a, b, dims=_NN):
    return lax.dot_general(a, b, dims, preferred_element_type=F32)


def _mm(a, b):
    return _dot(a.astype(BF16), b.astype(BF16))


def _cat0(*xs):
    return jnp.concatenate(xs, axis=0)


def _cat1(*xs):
    return jnp.concatenate(xs, axis=1)


def _const_spec(shape):
    nd = len(shape)
    return pl.BlockSpec(shape, lambda *_: (0,) * nd)


def _params(sem):
    return pltpu.CompilerParams(dimension_semantics=sem, vmem_limit_bytes=VMEM_LIMIT)


def _in_proj_body(tm, n_rwkv, n_s5, x_ref, g_ref, w32_ref, bg_ref, mu_ref, p_ref, u_ref, gate_ref, carry_ref,
                  w_ref):
    @pl.when((pl.program_id(0) == 0) & (pl.program_id(1) == 0))
    def _():
        for c0 in range(0, w_ref.shape[1], MXU_K):
            w_ref[:, c0:c0 + MXU_K] = w32_ref[:, c0:c0 + MXU_K].astype(BF16)

    @pl.when(pl.program_id(1) == 0)
    def _():
        carry_ref[...] = jnp.zeros_like(carry_ref)

    blocks = [slice(i * tm // ROW_SPLIT, (i + 1) * tm // ROW_SPLIT) for i in range(ROW_SPLIT)]
    rows = tm // ROW_SPLIT
    row = lax.broadcasted_iota(jnp.int32, (rows, 1), 0)
    last = carry_ref[...]
    for r in blocks:
        h = _rms(x_ref[0, r], g_ref[...]).astype(BF16)
        gate_ref[0, r] = jax.nn.sigmoid(
            jnp.dot(h, w_ref[:, n_rwkv + n_s5:], preferred_element_type=F32) + bg_ref[...]).astype(BF16)
        p = jnp.dot(h, w_ref[:, :n_rwkv], preferred_element_type=F32)
        prev = jnp.where(row == 0, last, pltpu.roll(p, 1, 0))
        last = p[rows - 1:rows, :]
        p_ref[0, r] = p + (prev - p) * mu_ref[...]
        u_ref[0, r] = jnp.dot(h, w_ref[:, n_rwkv:n_rwkv + n_s5], preferred_element_type=F32)
    carry_ref[...] = last


def _in_proj(x3, g, w_in, b_gate, mu, n_rwkv, n_s5, tm):
    b, l, d = x3.shape
    assert l % tm == 0
    n_in = w_in.shape[1]
    n_gate = n_in - n_rwkv - n_s5
    tile = lambda n: pl.BlockSpec((1, tm, n), lambda i, j: (i, j, 0))
    return pl.pallas_call(
        functools.partial(_in_proj_body, tm, n_rwkv, n_s5),
        grid=(b, l // tm),
        in_specs=[tile(d), _const_spec((1, d)), _const_spec((d, n_in)), _const_spec((1, n_gate)),
                  _const_spec((1, n_rwkv))],
        out_specs=[tile(n_rwkv), tile(n_s5), tile(n_gate)],
        out_shape=[jax.ShapeDtypeStruct((b, l, n_rwkv), F32), jax.ShapeDtypeStruct((b, l, n_s5), F32),
                   jax.ShapeDtypeStruct((b, l, n_gate), BF16)],
        scratch_shapes=[pltpu.VMEM((1, n_rwkv), F32), pltpu.VMEM((d, n_in), BF16)],
        compiler_params=_params(("arbitrary", "arbitrary")),
        name="in_proj",
    )(x3, g, w_in, b_gate, mu)


def _mstack(x, head_masks):
    z = jnp.zeros_like(x)
    return jnp.concatenate([jnp.where(m, x, z) for m in head_masks], axis=0)


def _wkv_stages(fetch, chunk_ids, hs, emit, masks):
    pair_masks, strict, incl, eye_mask = masks
    eye = eye_mask.astype(F32)
    c = CHUNK
    ms = lambda x: _mstack(x, pair_masks)
    b16 = lambda x: x.astype(BF16)
    n_pairs = len(hs)
    ids = [(k, j) for k in range(len(chunk_ids)) for j in range(n_pairs)]
    get = lambda name: [fetch(name, chunk_ids[k], j) for k, j in ids]
    n = range(len(ids))
    at, rt, bt, kt = get("at"), [b16(x) for x in get("rt")], get("bt"), get("kt")
    g = [_dot(_cat0(at[i], rt[i]), _cat0(ms(bt[i]), ms(kt[i])), _NT) for i in n]
    yield
    a_ab = [jnp.where(strict, x[:c, :LANES], 0.0) for x in g]
    a_ak = [b16(jnp.where(strict, x[:c, LANES:], 0.0)) for x in g]
    a_rbk = [b16(_cat1(jnp.where(incl, x[c:, :LANES], 0.0), jnp.where(incl, x[c:, LANES:], 0.0))) for x in g]
    ab16 = [b16(x) for x in a_ab]
    q = [b16(_dot(ab16[i], ms(ab16[i]))) for i in n]
    t_inv = [eye + x for x in a_ab]
    yield
    n_sq = int(math.log2(c)) - 1
    for s in range(n_sq - 1):
        tq = [_dot(_cat0(b16(t_inv[i]), q[i]), ms(q[i])) for i in n]
        t_inv = [t_inv[i] + tq[i][:c] for i in n]
        q = [b16(tq[i][c:]) for i in n]
        yield
    t16 = [b16(t_inv[i]) for i in n]
    t_inv = [b16(t_inv[i] + _dot(t16[i], ms(q[i]))) for i in n]
    msv = [ms(x) for x in get("v")]
    w0 = [b16(_dot(a_ak[i], msv[i])) for i in n]
    yield
    at = get("at")
    x1 = [_dot(t_inv[i], _cat1(ms(at[i]), ms(w0[i]))) for i in n]
    ap = [b16(x[:, :LANES]) for x in x1]
    u0 = [b16(x[:, LANES:]) for x in x1]
    yield
    zero = jnp.zeros((2 * c, LANES), BF16)
    ry = [_dot(a_rbk[i], _cat0(_cat1(ms(ap[i]), ms(u0[i])), _cat1(zero, msv[i]))) for i in n]
    yield
    bh, kh, v, rt32, pc = [get(name) for name in ("bh", "kh", "v", "rt", "pc")]
    mh = [_dot(_cat0(bh[i], kh[i]), _cat0(_cat1(ap[i], u0[i]), _cat1(zero[:c], v[i])), _TN) for i in n]
    rp = [b16(rt32[i] + ry[i][:, :LANES]) for i in n]
    yield
    blocks = lambda x: jnp.where(pair_masks[0], x[:c], x[c:])
    m = [b16(blocks(mh[i][:, :LANES]) + jnp.where(eye_mask, pc[i], 0.0)) for i in n]
    hadd = [blocks(mh[i][:, LANES:]) for i in n]

    def apply_state(k):
        for j in range(n_pairs):
            i = k * n_pairs + j
            yh = _dot(_cat0(rp[i], m[i]), ms(b16(hs[j])))
            emit(chunk_ids[k], j, yh[:c] + ry[i][:, LANES:])
            hs[j] = yh[c:] + hadd[i]

    return [functools.partial(apply_state, k) for k in range(len(chunk_ids))]


def _drive(stages, fillers, per_stage):
    while True:
        try:
            next(stages)
        except StopIteration as done:
            return done.value
        for _ in range(per_stage):
            if fillers:
                fillers.pop(0)()


def _rwkv_body(tl, width, p_ref, w0_ref, wwa_ref, a0_ref, g2_ref, kk_ref, ka_ref, rk_ref, lnw_ref, lnb_ref,
               ones_ref, y_ref, h_ref, pre_ref, twa_ref, g_ref, bonus_ref, ss_ref, rt_ref, ops_ref, pc_ref,
               yacc_ref):
    @pl.when(pl.program_id(1) == 0)
    def _():
        h_ref[...] = jnp.zeros_like(h_ref)

    c = CHUNK
    w3 = 3 * width
    n_pairs = width // LANES
    r_cols, k_cols, v_cols = slice(0, width), slice(width, 2 * width), slice(2 * width, w3)
    lane = lax.broadcasted_iota(jnp.int32, (1, LANES), 1)
    m0 = lane < HEAD
    ones = ones_ref[...]
    nk = ones.shape[0]
    segsum = lambda x: _cat1(*[_mm(x[:, i:i + nk], ones) for i in range(0, width, nk)])
    rowc = lax.broadcasted_iota(jnp.int32, (c, LANES), 0)
    colc = lax.broadcasted_iota(jnp.int32, (c, LANES), 1) % HEAD
    masks = ([m0, jnp.logical_not(m0)], colc < rowc, colc <= rowc, colc == rowc)
    ltri = (lax.broadcasted_iota(jnp.int32, (c, c), 1) <= lax.broadcasted_iota(jnp.int32, (c, c), 0)).astype(BF16)
    ltri2 = _cat1(ltri, ltri)

    def low_rank(rows):
        wa = p_ref[0, rows, w3:w3 + LANES]
        twa_ref[rows, :] = _mm(jnp.where(m0, jnp.tanh(wa), wa), wwa_ref[...])
        g_ref[rows, :] = _mm(jax.nn.sigmoid(p_ref[0, rows, w3 + LANES:]), g2_ref[...])

    def elementwise(rows):
        pre_ref[3, rows, :] = -math.exp(-0.5) * jax.nn.sigmoid(w0_ref[...] + twa_ref[rows, :width])
        a = jax.nn.sigmoid(a0_ref[...] + twa_ref[rows, width:])
        k = p_ref[0, rows, k_cols]
        k2 = k * (1.0 + (a - 1.0) * ka_ref[...])
        pre_ref[0, rows, :] = k2
        pre_ref[1, rows, :] = k * kk_ref[...]
        pre_ref[2, rows, :] = a
        bonus_ref[rows, :] = p_ref[0, rows, r_cols] * k2 * rk_ref[...]

    def head_sums(rows):
        kk = pre_ref[1, rows, :]
        ss_ref[rows, :] = segsum(kk * kk)
        bonus_ref[rows, :] = segsum(bonus_ref[rows, :])

    def normalise(rows):
        kk = pre_ref[1, rows, :] * lax.rsqrt(jnp.maximum(ss_ref[rows, :], 1e-24))
        pre_ref[2, rows, :] = kk * pre_ref[2, rows, :]
        pre_ref[1, rows, :] = -kk
        bonus_ref[rows, :] = bonus_ref[rows, :] * p_ref[0, rows, v_cols]

    def decay_scale(ci):
        rows = slice(ci * c, (ci + 1) * c)
        lwc = pre_ref[3, rows, :]
        cum = _dot(ltri2, _cat0(*_split(lwc)))
        pw = jnp.exp(cum)
        pinv = jnp.exp(-cum)
        kt = pre_ref[0, rows, :] * pinv
        bt = pre_ref[2, rows, :] * pinv
        pc = pw[c - 1:c, :]
        rt_ref[rows, :] = p_ref[0, rows, r_cols] * pw
        for i, x in enumerate((pre_ref[1, rows, :] * jnp.exp(cum - lwc), kt, bt, bt * pc, kt * pc,
                               p_ref[0, rows, v_cols])):
            ops_ref[i, rows, :] = x.astype(BF16)
        pc_ref[ci:ci + 1, :] = pc

    op_names = ("at", "kt", "bt", "bh", "kh", "v")

    def fetch(name, ci, j):
        rows, lanes = slice(ci * c, (ci + 1) * c), slice(j * LANES, (j + 1) * LANES)
        if name == "rt":
            return rt_ref[rows, lanes]
        if name == "pc":
            return pc_ref[ci:ci + 1, lanes]
        return ops_ref[op_names.index(name), rows, lanes]

    def centre(rows):
        y = yacc_ref[rows, :]
        yacc_ref[rows, :] = y - segsum(y) * (1.0 / HEAD)

    def group_norm(rows):
        yc = yacc_ref[rows, :]
        var = segsum(yc * yc) * (1.0 / HEAD)
        yn = yc * lax.rsqrt(var + LNX_EPS) * lnw_ref[...] + lnb_ref[...]
        y_ref[0, rows, :] = ((yn + bonus_ref[rows, :]) * g_ref[rows, :]).astype(BF16)

    gc = WKV_GROUP
    n_groups = tl // (c * gc)
    half = c * gc // 2

    def prepare(gi):
        r0 = gi * gc * c
        blocks = [slice(r0 + i * c, r0 + (i + 1) * c) for i in range(gc)]
        halves = [slice(r0, r0 + half), slice(r0 + half, r0 + 2 * half)]
        return ([functools.partial(low_rank, h) for h in halves] + [functools.partial(elementwise, b) for b in blocks]
                + [functools.partial(head_sums, h) for h in halves] + [functools.partial(normalise, b) for b in blocks]
                + [functools.partial(decay_scale, ci) for ci in range(gi * gc, (gi + 1) * gc)])

    def finish(gi):
        r0 = gi * gc * c
        halves = [slice(r0, r0 + half), slice(r0 + half, r0 + 2 * half)]
        return [functools.partial(f, h) for f in (centre, group_norm) for h in halves]

    def emit(ci, j, y):
        yacc_ref[ci * c:(ci + 1) * c, j * LANES:(j + 1) * LANES] = y

    for thunk in prepare(0):
        thunk()
    hs = [h_ref[j] for j in range(n_pairs)]
    state_steps = []
    for gi in range(n_groups):
        prep = prepare(gi + 1) if gi + 1 < n_groups else []
        fillers = []
        for step in state_steps:
            fillers += [step] + prep[:2]
            prep = prep[2:]
        fillers += prep + (finish(gi - 1) if gi > 0 else [])
        chunk_ids = list(range(gi * gc, (gi + 1) * gc))
        state_steps = _drive(_wkv_stages(fetch, chunk_ids, hs, emit, masks), fillers, FILL_PER_STAGE)
        for thunk in fillers:
            thunk()
    for thunk in state_steps + finish(n_groups - 1):
        thunk()
    for j in range(n_pairs):
        h_ref[j] = hs[j]


def _rwkv(p3, w0, wwa, a0, g2, k_k, k_a, r_k, lnx_w, lnx_b, ones_bd, width, tl):
    b, l, n_rwkv = p3.shape
    assert l % tl == 0 and tl % (CHUNK * WKV_GROUP) == 0 and (tl // CHUNK) % SUBLANES == 0
    n_pairs = width // LANES
    consts = [w0, wwa, a0, g2, k_k, k_a, r_k, lnx_w, lnx_b, ones_bd]
    rows = lambda n=width: pltpu.VMEM((tl, n), F32)
    return pl.pallas_call(
        functools.partial(_rwkv_body, tl, width),
        grid=(b, l // tl),
        in_specs=[pl.BlockSpec((1, tl, n_rwkv), lambda i, j: (i, j, 0))] + [_const_spec(c.shape) for c in consts],
        out_specs=pl.BlockSpec((1, tl, width), lambda i, j: (i, j, 0)),
        out_shape=jax.ShapeDtypeStruct((b, l, width), BF16),
        scratch_shapes=[pltpu.VMEM((n_pairs, HEAD, LANES), F32), pltpu.VMEM((4, tl, width), F32),
                        rows(2 * width), rows(), rows(), rows(), rows(), pltpu.VMEM((6, tl, width), BF16),
                        pltpu.VMEM((tl // CHUNK, width), F32), rows()],
        compiler_params=_params(("arbitrary", "arbitrary")),
        name="rwkv7",
    )(p3, *consts)


def _s5_disc_body(are_ref, aim_ref, dt_ref, bre_ref, bim_ref, cre_ref, cim_ref,
                  abr_ref, abi_ref, bbr_ref, bbi_ref, bar_ref, bai_ref, car_ref, cai_ref):
    lre, lim, dt = are_ref[...], aim_ref[...], jnp.exp(dt_ref[...])
    mag = jnp.exp(lre * dt)
    abr = mag * jnp.cos(lim * dt)
    abi = mag * jnp.sin(lim * dt)
    inv = 1.0 / (lre * lre + lim * lim)
    nr, ni = abr - 1.0, abi
    cr = (nr * lre + ni * lim) * inv
    ci = (ni * lre - nr * lim) * inv
    bre, bim = bre_ref[...], bim_ref[...]
    bbr = cr * bre - ci * bim
    bbi = cr * bim + ci * bre
    cre, cim = cre_ref[...], cim_ref[...]
    abr_ref[...] = abr
    abi_ref[...] = abi
    bbr_ref[...] = bbr
    bbi_ref[...] = bbi
    bar_ref[...] = bbr * abr - bbi * abi
    bai_ref[...] = bbr * abi + bbi * abr
    car_ref[...] = cre * abr - cim * abi
    cai_ref[...] = cre * abi + cim * abr


def _s5_disc(a_re, a_im, log_step, b_re, b_im, c_re, c_im):
    g, p, c = b_re.shape
    per_channel = lambda x: jnp.broadcast_to(x.reshape(g, 1, -1), (g, c, p)).reshape(g * c, p)
    rows = lambda x: x.transpose(0, 2, 1).reshape(g * c, p)
    outs = pl.pallas_call(
        _s5_disc_body,
        out_shape=[jax.ShapeDtypeStruct((g * c, p), F32)] * 8,
        name="s5_disc",
    )(per_channel(a_re), per_channel(a_im), per_channel(log_step), rows(b_re), rows(b_im),
      c_re.reshape(g * c, p), c_im.reshape(g * c, p))
    abr, abi, bbr, bbi, bar, bai, car, cai = [x.reshape(g, c, p) for x in outs]
    return abr[:, 0], abi[:, 0], (bbr, bbi), (bar, bai), (car, cai)


def _s5_body(tt, n_sub, nh, n_cast, u_ref, perm_ref, permt_ref, b2_ref, are_ref, aim_ref, c_ref, ca_ref, cb_ref,
             d_ref, wglu_ref, bglu_ref, *rest):
    w32_refs, y_ref, w16_refs = rest[:n_cast], rest[n_cast], rest[n_cast + 1:2 * n_cast + 1]
    st_ref, s_ref = rest[2 * n_cast + 1:]
    for w32_ref, w16_ref in zip(w32_refs, w16_refs):
        w16_ref[...] = w32_ref[...].astype(BF16)

    @pl.when(pl.program_id(0) == 0)
    def _():
        s_ref[...] = jnp.zeros_like(s_ref)

    nb = SUBLANES
    rows = nb * tt
    half = rows // 2
    width = u_ref.shape[-1]
    kin = width // nh
    nsh = are_ref.shape[-1]
    re_c, im_c = slice(0, nsh), slice(nsh, 2 * nsh)
    permt = permt_ref[...]
    permt2 = _cat1(permt, permt)
    us = [u_ref[:, i * tt:(i + 1) * tt, :].reshape(rows, width) for i in range(n_sub)]
    u_even = []
    for i in range(n_sub):
        u_tm = jnp.dot(perm_ref[...], us[i].astype(BF16), preferred_element_type=F32).astype(BF16)
        u_even.append(u_tm[:half])
        for h in range(nh):
            cols = slice(h * kin, (h + 1) * kin)
            st_ref[i, h, nb:, :] = jnp.dot(_cat1(u_tm[half:, cols], u_tm[:half, cols]), b2_ref[h],
                                           preferred_element_type=F32)
    for i in range(n_sub):
        for h in range(nh):
            are, aim = are_ref[h], aim_ref[h]
            a2re = jnp.broadcast_to(are * are - aim * aim, (nb, nsh))
            a2im = jnp.broadcast_to(2.0 * are * aim, (nb, nsh))
            st_ref[i, h, :nb, :] = s_ref[h]

            def step(k, carry):
                sre, sim = carry
                r8 = pl.ds(pl.multiple_of((k + 1) * nb, nb), nb)
                nre = a2re * sre - a2im * sim + st_ref[i, h, r8, re_c]
                nim = a2re * sim + a2im * sre + st_ref[i, h, r8, im_c]
                st_ref[i, h, r8, re_c] = nre
                st_ref[i, h, r8, im_c] = nim
                return nre, nim

            sre, sim = lax.fori_loop(0, tt // 2, step, (s_ref[h, :, re_c], s_ref[h, :, im_c]), unroll=True)
            s_ref[h, :, re_c] = sre
            s_ref[h, :, im_c] = sim
        y_odd = _cat1(*[jnp.dot(st_ref[i, h, nb:, :].astype(BF16), c_ref[h], preferred_element_type=F32)
                        for h in range(nh)])
        y_even = _cat1(*[jnp.dot(st_ref[i, h, :half, :].astype(BF16), ca_ref[h], preferred_element_type=F32)
                         + jnp.dot(u_even[i][:, h * kin:(h + 1) * kin], cb_ref[h], preferred_element_type=F32)
                         for h in range(nh)])
        y = _dot(permt2, _cat0(*_split(_cat0(y_even, y_odd))))
        y = jax.nn.gelu(y + d_ref[...] * us[i])
        y = y * jax.nn.sigmoid(jnp.dot(y.astype(BF16), wglu_ref[...], preferred_element_type=F32) + bglu_ref[...])
        y_ref[:, i * tt:(i + 1) * tt, :] = y.astype(BF16).reshape(nb, tt, width)


def _row_blocks(n_rows, n_steps):
    repeat = 1
    while n_rows % (n_steps // repeat) or (n_rows // (n_steps // repeat)) % (2 * SUBLANES):
        repeat *= 2
        assert repeat <= n_steps and n_steps % repeat == 0
    return n_rows // (n_steps // repeat), repeat


def _s5(u3, perm, b2_h, are, aim, c_h, ca_h, cb_h, d, w_glu, b_glu, tt, n_sub, cast):
    b, l, width = u3.shape
    assert l % (tt * n_sub) == 0 and b == SUBLANES and tt % 2 == 0
    nh, _, nsh2 = b2_h.shape
    n_steps = l // (tt * n_sub)
    consts = [perm, perm.T, b2_h, are, aim, c_h, ca_h, cb_h, d, w_glu, b_glu]
    blk = pl.BlockSpec((b, tt * n_sub, width), lambda i: (0, i, 0))

    def cast_spec(w):
        rows, repeat = _row_blocks(w.shape[0], n_steps)
        return pl.BlockSpec((rows, w.shape[1]), lambda i: (i // repeat, 0))

    outs = pl.pallas_call(
        functools.partial(_s5_body, tt, n_sub, nh, len(cast)),
        grid=(n_steps,),
        in_specs=[blk] + [_const_spec(c.shape) for c in consts] + [cast_spec(w) for w in cast],
        out_specs=[blk] + [cast_spec(w) for w in cast],
        out_shape=[jax.ShapeDtypeStruct((b, l, width), BF16)] + [jax.ShapeDtypeStruct(w.shape, BF16) for w in cast],
        scratch_shapes=[pltpu.VMEM((n_sub, nh, b + b * tt // 2, nsh2), F32), pltpu.VMEM((nh, b, nsh2), F32)],
        compiler_params=_params(("arbitrary",)),
        name="s5",
    )(u3, *consts, *cast)
    return outs[0], outs[1:]


def _mix_ffn_body(tm, d, f, cb, x_ref, yr_ref, ys_ref, gate_ref, wbr_ref, wbs_ref, wout_ref, gmix_ref,
                  gpre_ref, wup_ref, cw_ref, cbias_ref, wdown_ref, gpost_ref, o_ref, carry_ref, act_ref):
    @pl.when(pl.program_id(1) == 0)
    def _():
        carry_ref[...] = jnp.zeros_like(carry_ref)

    blocks = [slice(i * tm // ROW_SPLIT, (i + 1) * tm // ROW_SPLIT) for i in range(ROW_SPLIT)]
    mixed = [gate_ref[0, r, :d] * jnp.dot(yr_ref[0, r], wbr_ref[...], preferred_element_type=F32)
             + gate_ref[0, r, d:] * jnp.dot(ys_ref[0, r], wbs_ref[...], preferred_element_type=F32) for r in blocks]
    mixed = [jnp.dot(m.astype(BF16), wout_ref[...], preferred_element_type=F32) for m in mixed]
    xs = [x_ref[0, r] + _rms(m, gmix_ref[...]) for r, m in zip(blocks, mixed)]

    h = _cat0(*[_rms(x, gpre_ref[...]).astype(BF16) for x in xs])
    nb = SUBLANES
    sub = lax.broadcasted_iota(jnp.int32, (nb, 1), 0)

    def conv(cols):
        z = jnp.dot(h, wup_ref[:, cols], preferred_element_type=F32)
        tail = carry_ref[:, cols]
        carry_ref[:, cols] = z[tm - nb:, :]
        out = cbias_ref[:, cols] + cw_ref[CONV_WIDTH - 1:CONV_WIDTH, cols] * z
        for s in range(1, CONV_WIDTH):
            zr = pltpu.roll(z, s, 0)
            top = jnp.where(sub < s, pltpu.roll(tail, s, 0), zr[:nb])
            zs = jnp.concatenate([top, zr[nb:]], axis=0)
            out = out + cw_ref[CONV_WIDTH - 1 - s:CONV_WIDTH - s, cols] * zs
        return out

    for j in range(f // cb):
        gate = conv(slice(j * cb, (j + 1) * cb))
        val = conv(slice(f + j * cb, f + (j + 1) * cb))
        act_ref[:, j * cb:(j + 1) * cb] = (jax.nn.gelu(gate) * val).astype(BF16)
    ys = [jnp.dot(act_ref[r, :], wdown_ref[...], preferred_element_type=F32) for r in blocks]
    for r, x, y in zip(blocks, xs, ys):
        o_ref[0, r] = x + _rms(y, gpost_ref[...])


def _mix_ffn(x3, yr, ys, gates, wbr, wbs, wout, g_mix, g_pre, w_up, conv_w, conv_b, w_down, g_post, tm):
    b, l, d = x3.shape
    f = w_down.shape[0]
    cb = 2 * LANES
    assert l % tm == 0 and f % cb == 0
    consts = [wbr, wbs, wout, g_mix, g_pre, w_up, conv_w, conv_b, w_down, g_post]
    tile = lambda a: pl.BlockSpec((1, tm, a.shape[-1]), lambda i, j: (i, j, 0))
    acts = [x3, yr, ys, gates]
    return pl.pallas_call(
        functools.partial(_mix_ffn_body, tm, d, f, cb),
        grid=(b, l // tm),
        in_specs=[tile(a) for a in acts] + [_const_spec(c.shape) for c in consts],
        out_specs=tile(x3),
        out_shape=jax.ShapeDtypeStruct((b, l, d), F32),
        scratch_shapes=[pltpu.VMEM((SUBLANES, 2 * f), F32), pltpu.VMEM((tm, f), BF16)],
        compiler_params=_params(("arbitrary", "arbitrary")),
        name="mix_ffn",
    )(*acts, *consts)


def _block_diag(blocks):
    g, r, c = blocks.shape
    eye = jnp.eye(g, dtype=blocks.dtype)
    return (blocks[:, :, None, :] * eye[:, None, :, None]).reshape(g * r, g * c)


def _layer(x, norm_mix_pre, norm_mix_post, norm_ffn_pre, norm_ffn_post, w_in, b_gate, mu, w0, w2, a0, a2, g2,
           k_k, k_a, r_k, lnx_w, lnx_b, s5_a_re, s5_a_im, s5_b_re, s5_b_im, s5_c_re, s5_c_im, s5_d, s5_log_step,
           s5_w_glu, s5_b_glu, w_branch_rwkv, w_branch_s5, w_out, ffn_w_up, ffn_conv_w, ffn_conv_b, ffn_w_down,
           tiles):
    b, l, d = x.shape
    width = w0.shape[0]
    s5_width = s5_d.shape[0]
    n_rwkv = mu.shape[0]
    rank_w, rank_a = w2.shape[0], a2.shape[0]
    assert rank_w == HEAD and rank_a == HEAD and g2.shape[0] == LANES and b == SUBLANES
    rowv = lambda vec: vec.reshape(1, -1)
    p, u, gates = _in_proj(x, rowv(norm_mix_pre), w_in, rowv(b_gate), rowv(mu), n_rwkv, s5_width,
                           tiles["in"])

    wwa = jnp.zeros((LANES, 2 * width), F32).at[:rank_w, :width].set(w2).at[rank_w:, width:].set(a2)
    ones_bd = _block_diag(jnp.ones((MXU_K // HEAD, HEAD, HEAD), F32)).astype(BF16)
    y_rwkv = _rwkv(p, rowv(w0), wwa.astype(BF16), rowv(a0), g2.astype(BF16), rowv(k_k), rowv(k_a), rowv(r_k),
                   rowv(lnx_w), rowv(lnx_b), ones_bd, width, tiles["rwkv"])

    abr, abi, bb, ba, ca = _s5_disc(s5_a_re, s5_a_im, s5_log_step, s5_b_re, s5_b_im, s5_c_re, s5_c_im)
    n_grp = s5_a_re.shape[0]
    gh = LANES // S5_GROUP
    nh = n_grp // gh
    parts = lambda a: a.reshape((nh, gh) + a.shape[1:])
    bd = jax.vmap(_block_diag)
    tr = lambda a: parts(a).transpose(0, 1, 3, 2)
    re_im = lambda z: jnp.concatenate([bd(parts(z[0])), bd(parts(z[1]))], axis=2)
    b2_h = jnp.concatenate([re_im(bb), re_im(ba)], axis=1)
    c_h = jnp.concatenate([bd(tr(s5_c_re)), -bd(tr(s5_c_im))], axis=1)
    ca_h = jnp.concatenate([bd(tr(ca[0])), -bd(tr(ca[1]))], axis=1)
    cb = jnp.einsum("gdp,gcp->gcd", s5_c_re, bb[0]) - jnp.einsum("gdp,gcp->gcd", s5_c_im, bb[1])
    cb_h = bd(parts(cb))
    tt = tiles["s5"]
    ridx = jnp.arange(b * tt)
    step = 2 * ((ridx % (b * tt // 2)) // b) + ridx // (b * tt // 2)
    perm = (ridx[None, :] == ((ridx % b) * tt + step)[:, None]).astype(BF16)
    y_s5, (wbr, wbs, wout, wup, wdown) = _s5(
        u, perm, b2_h.astype(BF16), abr.reshape(nh, 1, -1), abi.reshape(nh, 1, -1), c_h.astype(BF16),
        ca_h.astype(BF16), cb_h.astype(BF16), rowv(s5_d), s5_w_glu.astype(BF16), rowv(s5_b_glu), tt,
        tiles["s5_sub"], cast=[w_branch_rwkv, w_branch_s5, w_out, ffn_w_up, ffn_w_down])

    return _mix_ffn(x, y_rwkv, y_s5, gates, wbr, wbs, wout, rowv(norm_mix_post), rowv(norm_ffn_pre), wup, ffn_conv_w,
                    rowv(ffn_conv_b), wdown, rowv(norm_ffn_post), tiles["ffn"])


TILES = {"in": 512, "rwkv": 1024, "s5": 32, "s5_sub": 8, "ffn": 512}


def kernel(x, norm_mix_pre, norm_mix_post, norm_ffn_pre, norm_ffn_post, w_in, b_gate, rwkv_shift_mu, rwkv_w0, rwkv_w2, rwkv_a0, rwkv_a2, rwkv_g2, rwkv_k_k, rwkv_k_a, rwkv_r_k, rwkv_lnx_w, rwkv_lnx_b, s5_a_re, s5_a_im, s5_b_re, s5_b_im, s5_c_re, s5_c_im, s5_d, s5_log_step, s5_w_glu, s5_b_glu, w_branch_rwkv, w_branch_s5, w_out, ffn_w_up, ffn_conv_w, ffn_conv_b, ffn_w_down):
    depth = w_in.shape[0]
    for i in range(depth):
        x = _layer(x, norm_mix_pre[i], norm_mix_post[i], norm_ffn_pre[i], norm_ffn_post[i], w_in[i], b_gate[i],
                   rwkv_shift_mu[i], rwkv_w0[i], rwkv_w2[i], rwkv_a0[i], rwkv_a2[i], rwkv_g2[i], rwkv_k_k[i],
                   rwkv_k_a[i], rwkv_r_k[i].reshape(-1), rwkv_lnx_w[i], rwkv_lnx_b[i], s5_a_re[i], s5_a_im[i],
                   s5_b_re[i], s5_b_im[i], s5_c_re[i], s5_c_im[i], s5_d[i], s5_log_step[i], s5_w_glu[i],
                   s5_b_glu[i], w_branch_rwkv[i], w_branch_s5[i], w_out[i], ffn_w_up[i], ffn_conv_w[i],
                   ffn_conv_b[i], ffn_w_down[i], TILES)
    return x
```

```python
import functools
import math

import jax
import jax.numpy as jnp
from jax import lax
from jax.experimental import pallas as pl
from jax.experimental.pallas import tpu as pltpu

F32 = jnp.float32
BF16 = jnp.bfloat16

NORM_EPS = 1e-6
LNX_EPS = 64e-5
HEAD = 64
CHUNK = 64
WKV_GROUP = 4
FILL_PER_STAGE = 3
LANES = 128
SUBLANES = 8
S5_GROUP = 16
S5_STATE = 64
CONV_WIDTH = 3
ROW_SPLIT = 2
MXU_K = 256
VMEM_LIMIT = 56 * 1024 * 1024


def _rms(x, g):
    return x * lax.rsqrt(jnp.mean(x * x, axis=-1, keepdims=True) + NORM_EPS) * g


def _split(x):
    hi = x.astype(BF16)
    lo = (x - hi.astype(F32)).astype(BF16)
    return hi, lo


_NN = (((1,), (0,)), ((), ()))
_NT = (((1,), (1,)), ((), ()))
_TN = (((0,), (0,)), ((), ()))


def _dot(a, b, dims=_NN):
    return lax.dot_general(a, b, dims, preferred_element_type=F32)


def _mm(a, b):
    return _dot(a.astype(BF16), b.astype(BF16))


def _cat0(*xs):
    return jnp.concatenate(xs, axis=0)


def _cat1(*xs):
    return jnp.concatenate(xs, axis=1)


def _const_spec(shape):
    nd = len(shape)
    return pl.BlockSpec(shape, lambda *_: (0,) * nd)


def _params(sem):
    return pltpu.CompilerParams(dimension_semantics=sem, vmem_limit_bytes=VMEM_LIMIT)


def _in_proj_body(tm, n_rwkv, n_s5, x_ref, g_ref, w32_ref, bg_ref, mu_ref, p_ref, u_ref, gate_ref, carry_ref,
                  w_ref):
    @pl.when((pl.program_id(0) == 0) & (pl.program_id(1) == 0))
    def _():
        for c0 in range(0, w_ref.shape[1], MXU_K):
            w_ref[:, c0:c0 + MXU_K] = w32_ref[:, c0:c0 + MXU_K].astype(BF16)

    @pl.when(pl.program_id(1) == 0)
    def _():
        carry_ref[...] = jnp.zeros_like(carry_ref)

    blocks = [slice(i * tm // ROW_SPLIT, (i + 1) * tm // ROW_SPLIT) for i in range(ROW_SPLIT)]
    rows = tm // ROW_SPLIT
    row = lax.broadcasted_iota(jnp.int32, (rows, 1), 0)
    last = carry_ref[...]
    for r in blocks:
        h = _rms(x_ref[0, r], g_ref[...]).astype(BF16)
        gate_ref[0, r] = jax.nn.sigmoid(
            jnp.dot(h, w_ref[:, n_rwkv + n_s5:], preferred_element_type=F32) + bg_ref[...]).astype(BF16)
        p = jnp.dot(h, w_ref[:, :n_rwkv], preferred_element_type=F32)
        prev = jnp.where(row == 0, last, pltpu.roll(p, 1, 0))
        last = p[rows - 1:rows, :]
        p_ref[0, r] = p + (prev - p) * mu_ref[...]
        u_ref[0, r] = jnp.dot(h, w_ref[:, n_rwkv:n_rwkv + n_s5], preferred_element_type=F32)
    carry_ref[...] = last


def _in_proj(x3, g, w_in, b_gate, mu, n_rwkv, n_s5, tm):
    b, l, d = x3.shape
    assert l % tm == 0
    n_in = w_in.shape[1]
    n_gate = n_in - n_rwkv - n_s5
    tile = lambda n: pl.BlockSpec((1, tm, n), lambda i, j: (i, j, 0))
    return pl.pallas_call(
        functools.partial(_in_proj_body, tm, n_rwkv, n_s5),
        grid=(b, l // tm),
        in_specs=[tile(d), _const_spec((1, d)), _const_spec((d, n_in)), _const_spec((1, n_gate)),
                  _const_spec((1, n_rwkv))],
        out_specs=[tile(n_rwkv), tile(n_s5), tile(n_gate)],
        out_shape=[jax.ShapeDtypeStruct((b, l, n_rwkv), F32), jax.ShapeDtypeStruct((b, l, n_s5), F32),
                   jax.ShapeDtypeStruct((b, l, n_gate), BF16)],
        scratch_shapes=[pltpu.VMEM((1, n_rwkv), F32), pltpu.VMEM((d, n_in), BF16)],
        compiler_params=_params(("arbitrary", "arbitrary")),
        name="in_proj",
    )(x3, g, w_in, b_gate, mu)


def _mstack(x, head_masks):
    z = jnp.zeros_like(x)
    return jnp.concatenate([jnp.where(m, x, z) for m in head_masks], axis=0)


def _wkv_stages(fetch, chunk_ids, hs, emit, masks):
    pair_masks, strict, incl, eye_mask = masks
    eye = eye_mask.astype(F32)
    c = CHUNK
    ms = lambda x: _mstack(x, pair_masks)
    b16 = lambda x: x.astype(BF16)
    n_pairs = len(hs)
    ids = [(k, j) for k in range(len(chunk_ids)) for j in range(n_pairs)]
    get = lambda name: [fetch(name, chunk_ids[k], j) for k, j in ids]
    n = range(len(ids))
    at, rt, bt, kt = get("at"), [b16(x) for x in get("rt")], get("bt"), get("kt")
    g = [_dot(_cat0(at[i], rt[i]), _cat0(ms(bt[i]), ms(kt[i])), _NT) for i in n]
    yield
    a_ab = [jnp.where(strict, x[:c, :LANES], 0.0) for x in g]
    a_ak = [b16(jnp.where(strict, x[:c, LANES:], 0.0)) for x in g]
    a_rbk = [b16(_cat1(jnp.where(incl, x[c:, :LANES], 0.0), jnp.where(incl, x[c:, LANES:], 0.0))) for x in g]
    ab16 = [b16(x) for x in a_ab]
    q = [b16(_dot(ab16[i], ms(ab16[i]))) for i in n]
    t_inv = [eye + x for x in a_ab]
    yield
    n_sq = int(math.log2(c)) - 1
    for s in range(n_sq - 1):
        tq = [_dot(_cat0(b16(t_inv[i]), q[i]), ms(q[i])) for i in n]
        t_inv = [t_inv[i] + tq[i][:c] for i in n]
        q = [b16(tq[i][c:]) for i in n]
        yield
    t16 = [b16(t_inv[i]) for i in n]
    t_inv = [b16(t_inv[i] + _dot(t16[i], ms(q[i]))) for i in n]
    msv = [ms(x) for x in get("v")]
    w0 = [b16(_dot(a_ak[i], msv[i])) for i in n]
    yield
    at = get("at")
    x1 = [_dot(t_inv[i], _cat1(ms(at[i]), ms(w0[i]))) for i in n]
    ap = [b16(x[:, :LANES]) for x in x1]
    u0 = [b16(x[:, LANES:]) for x in x1]
    yield
    zero = jnp.zeros((2 * c, LANES), BF16)
    ry = [_dot(a_rbk[i], _cat0(_cat1(ms(ap[i]), ms(u0[i])), _cat1(zero, msv[i]))) for i in n]
    yield
    bh, kh, v, rt32, pc = [get(name) for name in ("bh", "kh", "v", "rt", "pc")]
    mh = [_dot(_cat0(bh[i], kh[i]), _cat0(_cat1(ap[i], u0[i]), _cat1(zero[:c], v[i])), _TN) for i in n]
    rp = [b16(rt32[i] + ry[i][:, :LANES]) for i in n]
    yield
    blocks = lambda x: jnp.where(pair_masks[0], x[:c], x[c:])
    m = [b16(blocks(mh[i][:, :LANES]) + jnp.where(eye_mask, pc[i], 0.0)) for i in n]
    hadd = [blocks(mh[i][:, LANES:]) for i in n]

    def apply_state(k):
        for j in range(n_pairs):
            i = k * n_pairs + j
            yh = _dot(_cat0(rp[i], m[i]), ms(b16(hs[j])))
            emit(chunk_ids[k], j, yh[:c] + ry[i][:, LANES:])
            hs[j] = yh[c:] + hadd[i]

    return [functools.partial(apply_state, k) for k in range(len(chunk_ids))]


def _drive(stages, fillers, per_stage):
    while True:
        try:
            next(stages)
        except StopIteration as done:
            return done.value
        for _ in range(per_stage):
            if fillers:
                fillers.pop(0)()


def _rwkv_body(tl, width, p_ref, w0_ref, wwa_ref, a0_ref, g2_ref, kk_ref, ka_ref, rk_ref, lnw_ref, lnb_ref,
               ones_ref, y_ref, h_ref, pre_ref, twa_ref, g_ref, bonus_ref, ss_ref, rt_ref, ops_ref, pc_ref,
               yacc_ref):
    @pl.when(pl.program_id(1) == 0)
    def _():
        h_ref[...] = jnp.zeros_like(h_ref)

    c = CHUNK
    w3 = 3 * width
    n_pairs = width // LANES
    r_cols, k_cols, v_cols = slice(0, width), slice(width, 2 * width), slice(2 * width, w3)
    lane = lax.broadcasted_iota(jnp.int32, (1, LANES), 1)
    m0 = lane < HEAD
    ones = ones_ref[...]
    nk = ones.shape[0]
    segsum = lambda x: _cat1(*[_mm(x[:, i:i + nk], ones) for i in range(0, width, nk)])
    rowc = lax.broadcasted_iota(jnp.int32, (c, LANES), 0)
    colc = lax.broadcasted_iota(jnp.int32, (c, LANES), 1) % HEAD
    masks = ([m0, jnp.logical_not(m0)], colc < rowc, colc <= rowc, colc == rowc)
    ltri = (lax.broadcasted_iota(jnp.int32, (c, c), 1) <= lax.broadcasted_iota(jnp.int32, (c, c), 0)).astype(BF16)
    ltri2 = _cat1(ltri, ltri)

    def low_rank(rows):
        wa = p_ref[0, rows, w3:w3 + LANES]
        twa_ref[rows, :] = _mm(jnp.where(m0, jnp.tanh(wa), wa), wwa_ref[...])
        g_ref[rows, :] = _mm(jax.nn.sigmoid(p_ref[0, rows, w3 + LANES:]), g2_ref[...])

    def elementwise(rows):
        pre_ref[3, rows, :] = -math.exp(-0.5) * jax.nn.sigmoid(w0_ref[...] + twa_ref[rows, :width])
        a = jax.nn.sigmoid(a0_ref[...] + twa_ref[rows, width:])
        k = p_ref[0, rows, k_cols]
        k2 = k * (1.0 + (a - 1.0) * ka_ref[...])
        pre_ref[0, rows, :] = k2
        pre_ref[1, rows, :] = k * kk_ref[...]
        pre_ref[2, rows, :] = a
        bonus_ref[rows, :] = p_ref[0, rows, r_cols] * k2 * rk_ref[...]

    def head_sums(rows):
        kk = pre_ref[1, rows, :]
        ss_ref[rows, :] = segsum(kk * kk)
        bonus_ref[rows, :] = segsum(bonus_ref[rows, :])

    def normalise(rows):
        kk = pre_ref[1, rows, :] * lax.rsqrt(jnp.maximum(ss_ref[rows, :], 1e-24))
        pre_ref[2, rows, :] = kk * pre_ref[2, rows, :]
        pre_ref[1, rows, :] = -kk
        bonus_ref[rows, :] = bonus_ref[rows, :] * p_ref[0, rows, v_cols]

    def decay_scale(ci):
        rows = slice(ci * c, (ci + 1) * c)
        lwc = pre_ref[3, rows, :]
        cum = _dot(ltri2, _cat0(*_split(lwc)))
        pw = jnp.exp(cum)
        pinv = jnp.exp(-cum)
        kt = pre_ref[0, rows, :] * pinv
        bt = pre_ref[2, rows, :] * pinv
        pc = pw[c - 1:c, :]
        rt_ref[rows, :] = p_ref[0, rows, r_cols] * pw
        for i, x in enumerate((pre_ref[1, rows, :] * jnp.exp(cum - lwc), kt, bt, bt * pc, kt * pc,
                               p_ref[0, rows, v_cols])):
            ops_ref[i, rows, :] = x.astype(BF16)
        pc_ref[ci:ci + 1, :] = pc

    op_names = ("at", "kt", "bt", "bh", "kh", "v")

    def fetch(name, ci, j):
        rows, lanes = slice(ci * c, (ci + 1) * c), slice(j * LANES, (j + 1) * LANES)
        if name == "rt":
            return rt_ref[rows, lanes]
        if name == "pc":
            return pc_ref[ci:ci + 1, lanes]
        return ops_ref[op_names.index(name), rows, lanes]

    def centre(rows):
        y = yacc_ref[rows, :]
        yacc_ref[rows, :] = y - segsum(y) * (1.0 / HEAD)

    def group_norm(rows):
        yc = yacc_ref[rows, :]
        var = segsum(yc * yc) * (1.0 / HEAD)
        yn = yc * lax.rsqrt(var + LNX_EPS) * lnw_ref[...] + lnb_ref[...]
        y_ref[0, rows, :] = ((yn + bonus_ref[rows, :]) * g_ref[rows, :]).astype(BF16)

    gc = WKV_GROUP
    n_groups = tl // (c * gc)
    half = c * gc // 2

    def prepare(gi):
        r0 = gi * gc * c
        blocks = [slice(r0 + i * c, r0 + (i + 1) * c) for i in range(gc)]
        halves = [slice(r0, r0 + half), slice(r0 + half, r0 + 2 * half)]
        return ([functools.partial(low_rank, h) for h in halves] + [functools.partial(elementwise, b) for b in blocks]
                + [functools.partial(head_sums, h) for h in halves] + [functools.partial(normalise, b) for b in blocks]
                + [functools.partial(decay_scale, ci) for ci in range(gi * gc, (gi + 1) * gc)])

    def finish(gi):
        r0 = gi * gc * c
        halves = [slice(r0, r0 + half), slice(r0 + half, r0 + 2 * half)]
        return [functools.partial(f, h) for f in (centre, group_norm) for h in halves]

    def emit(ci, j, y):
        yacc_ref[ci * c:(ci + 1) * c, j * LANES:(j + 1) * LANES] = y

    for thunk in prepare(0):
        thunk()
    hs = [h_ref[j] for j in range(n_pairs)]
    state_steps = []
    for gi in range(n_groups):
        prep = prepare(gi + 1) if gi + 1 < n_groups else []
        fillers = []
        for step in state_steps:
            fillers += [step] + prep[:2]
            prep = prep[2:]
        fillers += prep + (finish(gi - 1) if gi > 0 else [])
        chunk_ids = list(range(gi * gc, (gi + 1) * gc))
        state_steps = _drive(_wkv_stages(fetch, chunk_ids, hs, emit, masks), fillers, FILL_PER_STAGE)
        for thunk in fillers:
            thunk()
    for thunk in state_steps + finish(n_groups - 1):
        thunk()
    for j in range(n_pairs):
        h_ref[j] = hs[j]


def _rwkv(p3, w0, wwa, a0, g2, k_k, k_a, r_k, lnx_w, lnx_b, ones_bd, width, tl):
    b, l, n_rwkv = p3.shape
    assert l % tl == 0 and tl % (CHUNK * WKV_GROUP) == 0 and (tl // CHUNK) % SUBLANES == 0
    n_pairs = width // LANES
    consts = [w0, wwa, a0, g2, k_k, k_a, r_k, lnx_w, lnx_b, ones_bd]
    rows = lambda n=width: pltpu.VMEM((tl, n), F32)
    return pl.pallas_call(
        functools.partial(_rwkv_body, tl, width),
        grid=(b, l // tl),
        in_specs=[pl.BlockSpec((1, tl, n_rwkv), lambda i, j: (i, j, 0))] + [_const_spec(c.shape) for c in consts],
        out_specs=pl.BlockSpec((1, tl, width), lambda i, j: (i, j, 0)),
        out_shape=jax.ShapeDtypeStruct((b, l, width), BF16),
        scratch_shapes=[pltpu.VMEM((n_pairs, HEAD, LANES), F32), pltpu.VMEM((4, tl, width), F32),
                        rows(2 * width), rows(), rows(), rows(), rows(), pltpu.VMEM((6, tl, width), BF16),
                        pltpu.VMEM((tl // CHUNK, width), F32), rows()],
        compiler_params=_params(("arbitrary", "arbitrary")),
        name="rwkv7",
    )(p3, *consts)


def _s5_disc_body(are_ref, aim_ref, dt_ref, bre_ref, bim_ref, cre_ref, cim_ref,
                  abr_ref, abi_ref, bbr_ref, bbi_ref, bar_ref, bai_ref, car_ref, cai_ref):
    lre, lim, dt = are_ref[...], aim_ref[...], jnp.exp(dt_ref[...])
    mag = jnp.exp(lre * dt)
    abr = mag * jnp.cos(lim * dt)
    abi = mag * jnp.sin(lim * dt)
    inv = 1.0 / (lre * lre + lim * lim)
    nr, ni = abr - 1.0, abi
    cr = (nr * lre + ni * lim) * inv
    ci = (ni * lre - nr * lim) * inv
    bre, bim = bre_ref[...], bim_ref[...]
    bbr = cr * bre - ci * bim
    bbi = cr * bim + ci * bre
    cre, cim = cre_ref[...], cim_ref[...]
    abr_ref[...] = abr
    abi_ref[...] = abi
    bbr_ref[...] = bbr
    bbi_ref[...] = bbi
    bar_ref[...] = bbr * abr - bbi * abi
    bai_ref[...] = bbr * abi + bbi * abr
    car_ref[...] = cre * abr - cim * abi
    cai_ref[...] = cre * abi + cim * abr


def _s5_disc(a_re, a_im, log_step, b_re, b_im, c_re, c_im):
    g, p, c = b_re.shape
    per_channel = lambda x: jnp.broadcast_to(x.reshape(g, 1, -1), (g, c, p)).reshape(g * c, p)
    rows = lambda x: x.transpose(0, 2, 1).reshape(g * c, p)
    outs = pl.pallas_call(
        _s5_disc_body,
        out_shape=[jax.ShapeDtypeStruct((g * c, p), F32)] * 8,
        name="s5_disc",
    )(per_channel(a_re), per_channel(a_im), per_channel(log_step), rows(b_re), rows(b_im),
      c_re.reshape(g * c, p), c_im.reshape(g * c, p))
    abr, abi, bbr, bbi, bar, bai, car, cai = [x.reshape(g, c, p) for x in outs]
    return abr[:, 0], abi[:, 0], (bbr, bbi), (bar, bai), (car, cai)


def _s5_body(tt, n_sub, nh, n_cast, u_ref, perm_ref, permt_ref, b2_ref, are_ref, aim_ref, c_ref, ca_ref, cb_ref,
             d_ref, wglu_ref, bglu_ref, *rest):
    w32_refs, y_ref, w16_refs = rest[:n_cast], rest[n_cast], rest[n_cast + 1:2 * n_cast + 1]
    st_ref, s_ref = rest[2 * n_cast + 1:]
    for w32_ref, w16_ref in zip(w32_refs, w16_refs):
        w16_ref[...] = w32_ref[...].astype(BF16)

    @pl.when(pl.program_id(0) == 0)
    def _():
        s_ref[...] = jnp.zeros_like(s_ref)

    nb = SUBLANES
    rows = nb * tt
    half = rows // 2
    width = u_ref.shape[-1]
    kin = width // nh
    nsh = are_ref.shape[-1]
    re_c, im_c = slice(0, nsh), slice(nsh, 2 * nsh)
    permt = permt_ref[...]
    permt2 = _cat1(permt, permt)
    us = [u_ref[:, i * tt:(i + 1) * tt, :].reshape(rows, width) for i in range(n_sub)]
    u_even = []
    for i in range(n_sub):
        u_tm = jnp.dot(perm_ref[...], us[i].astype(BF16), preferred_element_type=F32).astype(BF16)
        u_even.append(u_tm[:half])
        for h in range(nh):
            cols = slice(h * kin, (h + 1) * kin)
            st_ref[i, h, nb:, :] = jnp.dot(_cat1(u_tm[half:, cols], u_tm[:half, cols]), b2_ref[h],
                                           preferred_element_type=F32)
    for i in range(n_sub):
        for h in range(nh):
            are, aim = are_ref[h], aim_ref[h]
            a2re = jnp.broadcast_to(are * are - aim * aim, (nb, nsh))
            a2im = jnp.broadcast_to(2.0 * are * aim, (nb, nsh))
            st_ref[i, h, :nb, :] = s_ref[h]

            def step(k, carry):
                sre, sim = carry
                r8 = pl.ds(pl.multiple_of((k + 1) * nb, nb), nb)
                nre = a2re * sre - a2im * sim + st_ref[i, h, r8, re_c]
                nim = a2re * sim + a2im * sre + st_ref[i, h, r8, im_c]
                st_ref[i, h, r8, re_c] = nre
                st_ref[i, h, r8, im_c] = nim
                return nre, nim

            sre, sim = lax.fori_loop(0, tt // 2, step, (s_ref[h, :, re_c], s_ref[h, :, im_c]), unroll=True)
            s_ref[h, :, re_c] = sre
            s_ref[h, :, im_c] = sim
        y_odd = _cat1(*[jnp.dot(st_ref[i, h, nb:, :].astype(BF16), c_ref[h], preferred_element_type=F32)
                        for h in range(nh)])
        y_even = _cat1(*[jnp.dot(st_ref[i, h, :half, :].astype(BF16), ca_ref[h], preferred_element_type=F32)
                         + jnp.dot(u_even[i][:, h * kin:(h + 1) * kin], cb_ref[h], preferred_element_type=F32)
                         for h in range(nh)])
        y = _dot(permt2, _cat0(*_split(_cat0(y_even, y_odd))))
        y = jax.nn.gelu(y + d_ref[...] * us[i])
        y = y * jax.nn.sigmoid(jnp.dot(y.astype(BF16), wglu_ref[...], preferred_element_type=F32) + bglu_ref[...])
        y_ref[:, i * tt:(i + 1) * tt, :] = y.astype(BF16).reshape(nb, tt, width)


def _row_blocks(n_rows, n_steps):
    repeat = 1
    while n_rows % (n_steps // repeat) or (n_rows // (n_steps // repeat)) % (2 * SUBLANES):
        repeat *= 2
        assert repeat <= n_steps and n_steps % repeat == 0
    return n_rows // (n_steps // repeat), repeat


def _s5(u3, perm, b2_h, are, aim, c_h, ca_h, cb_h, d, w_glu, b_glu, tt, n_sub, cast):
    b, l, width = u3.shape
    assert l % (tt * n_sub) == 0 and b == SUBLANES and tt % 2 == 0
    nh, _, nsh2 = b2_h.shape
    n_steps = l // (tt * n_sub)
    consts = [perm, perm.T, b2_h, are, aim, c_h, ca_h, cb_h, d, w_glu, b_glu]
    blk = pl.BlockSpec((b, tt * n_sub, width), lambda i: (0, i, 0))

    def cast_spec(w):
        rows, repeat = _row_blocks(w.shape[0], n_steps)
        return pl.BlockSpec((rows, w.shape[1]), lambda i: (i // repeat, 0))

    outs = pl.pallas_call(
        functools.partial(_s5_body, tt, n_sub, nh, len(cast)),
        grid=(n_steps,),
        in_specs=[blk] + [_const_spec(c.shape) for c in consts] + [cast_spec(w) for w in cast],
        out_specs=[blk] + [cast_spec(w) for w in cast],
        out_shape=[jax.ShapeDtypeStruct((b, l, width), BF16)] + [jax.ShapeDtypeStruct(w.shape, BF16) for w in cast],
        scratch_shapes=[pltpu.VMEM((n_sub, nh, b + b * tt // 2, nsh2), F32), pltpu.VMEM((nh, b, nsh2), F32)],
        compiler_params=_params(("arbitrary",)),
        name="s5",
    )(u3, *consts, *cast)
    return outs[0], outs[1:]


def _mix_ffn_body(tm, d, f, cb, x_ref, yr_ref, ys_ref, gate_ref, wbr_ref, wbs_ref, wout_ref, gmix_ref,
                  gpre_ref, wup_ref, cw_ref, cbias_ref, wdown_ref, gpost_ref, o_ref, carry_ref, act_ref):
    @pl.when(pl.program_id(1) == 0)
    def _():
        carry_ref[...] = jnp.zeros_like(carry_ref)

    blocks = [slice(i * tm // ROW_SPLIT, (i + 1) * tm // ROW_SPLIT) for i in range(ROW_SPLIT)]
    mixed = [gate_ref[0, r, :d] * jnp.dot(yr_ref[0, r], wbr_ref[...], preferred_element_type=F32)
             + gate_ref[0, r, d:] * jnp.dot(ys_ref[0, r], wbs_ref[...], preferred_element_type=F32) for r in blocks]
    mixed = [jnp.dot(m.astype(BF16), wout_ref[...], preferred_element_type=F32) for m in mixed]
    xs = [x_ref[0, r] + _rms(m, gmix_ref[...]) for r, m in zip(blocks, mixed)]

    h = _cat0(*[_rms(x, gpre_ref[...]).astype(BF16) for x in xs])
    nb = SUBLANES
    sub = lax.broadcasted_iota(jnp.int32, (nb, 1), 0)

    def conv(z, cols):
        tail = carry_ref[:, cols]
        carry_ref[:, cols] = z[tm - nb:, :]
        out = cbias_ref[:, cols] + cw_ref[CONV_WIDTH - 1:CONV_WIDTH, cols] * z
        for s in range(1, CONV_WIDTH):
            zr = pltpu.roll(z, s, 0)
            top = jnp.where(sub < s, pltpu.roll(tail, s, 0), zr[:nb])
            zs = jnp.concatenate([top, zr[nb:]], axis=0)
            out = out + cw_ref[CONV_WIDTH - 1 - s:CONV_WIDTH - s, cols] * zs
        return out

    for j in range(f // cb):
        zg = jnp.dot(h, wup_ref[:, j * cb:(j + 1) * cb], preferred_element_type=F32)
        zv = jnp.dot(h, wup_ref[:, f + j * cb:f + (j + 1) * cb], preferred_element_type=F32)
        for k in range(0, cb, LANES):
            c0 = j * cb + k
            gate = conv(zg[:, k:k + LANES], slice(c0, c0 + LANES))
            val = conv(zv[:, k:k + LANES], slice(f + c0, f + c0 + LANES))
            act_ref[:, c0:c0 + LANES] = (jax.nn.gelu(gate) * val).astype(BF16)
    ys = [jnp.dot(act_ref[r, :], wdown_ref[...], preferred_element_type=F32) for r in blocks]
    for r, x, y in zip(blocks, xs, ys):
        o_ref[0, r] = x + _rms(y, gpost_ref[...])


def _mix_ffn(x3, yr, ys, gates, wbr, wbs, wout, g_mix, g_pre, w_up, conv_w, conv_b, w_down, g_post, tm):
    b, l, d = x3.shape
    f = w_down.shape[0]
    cb = 2 * LANES
    assert l % tm == 0 and f % cb == 0
    consts = [wbr, wbs, wout, g_mix, g_pre, w_up, conv_w, conv_b, w_down, g_post]
    tile = lambda a: pl.BlockSpec((1, tm, a.shape[-1]), lambda i, j: (i, j, 0))
    acts = [x3, yr, ys, gates]
    return pl.pallas_call(
        functools.partial(_mix_ffn_body, tm, d, f, cb),
        grid=(b, l // tm),
        in_specs=[tile(a) for a in acts] + [_const_spec(c.shape) for c in consts],
        out_specs=tile(x3),
        out_shape=jax.ShapeDtypeStruct((b, l, d), F32),
        scratch_shapes=[pltpu.VMEM((SUBLANES, 2 * f), F32), pltpu.VMEM((tm, f), BF16)],
        compiler_params=_params(("arbitrary", "arbitrary")),
        name="mix_ffn",
    )(*acts, *consts)


def _block_diag(blocks):
    g, r, c = blocks.shape
    eye = jnp.eye(g, dtype=blocks.dtype)
    return (blocks[:, :, None, :] * eye[:, None, :, None]).reshape(g * r, g * c)


def _layer(x, norm_mix_pre, norm_mix_post, norm_ffn_pre, norm_ffn_post, w_in, b_gate, mu, w0, w2, a0, a2, g2,
           k_k, k_a, r_k, lnx_w, lnx_b, s5_a_re, s5_a_im, s5_b_re, s5_b_im, s5_c_re, s5_c_im, s5_d, s5_log_step,
           s5_w_glu, s5_b_glu, w_branch_rwkv, w_branch_s5, w_out, ffn_w_up, ffn_conv_w, ffn_conv_b, ffn_w_down,
           tiles):
    b, l, d = x.shape
    width = w0.shape[0]
    s5_width = s5_d.shape[0]
    n_rwkv = mu.shape[0]
    rank_w, rank_a = w2.shape[0], a2.shape[0]
    assert rank_w == HEAD and rank_a == HEAD and g2.shape[0] == LANES and b == SUBLANES
    rowv = lambda vec: vec.reshape(1, -1)
    p, u, gates = _in_proj(x, rowv(norm_mix_pre), w_in, rowv(b_gate), rowv(mu), n_rwkv, s5_width,
                           tiles["in"])

    wwa = jnp.zeros((LANES, 2 * width), F32).at[:rank_w, :width].set(w2).at[rank_w:, width:].set(a2)
    ones_bd = _block_diag(jnp.ones((MXU_K // HEAD, HEAD, HEAD), F32)).astype(BF16)
    y_rwkv = _rwkv(p, rowv(w0), wwa.astype(BF16), rowv(a0), g2.astype(BF16), rowv(k_k), rowv(k_a), rowv(r_k),
                   rowv(lnx_w), rowv(lnx_b), ones_bd, width, tiles["rwkv"])

    abr, abi, bb, ba, ca = _s5_disc(s5_a_re, s5_a_im, s5_log_step, s5_b_re, s5_b_im, s5_c_re, s5_c_im)
    n_grp = s5_a_re.shape[0]
    gh = LANES // S5_GROUP
    nh = n_grp // gh
    parts = lambda a: a.reshape((nh, gh) + a.shape[1:])
    bd = jax.vmap(_block_diag)
    tr = lambda a: parts(a).transpose(0, 1, 3, 2)
    re_im = lambda z: jnp.concatenate([bd(parts(z[0])), bd(parts(z[1]))], axis=2)
    b2_h = jnp.concatenate([re_im(bb), re_im(ba)], axis=1)
    c_h = jnp.concatenate([bd(tr(s5_c_re)), -bd(tr(s5_c_im))], axis=1)
    ca_h = jnp.concatenate([bd(tr(ca[0])), -bd(tr(ca[1]))], axis=1)
    cb = jnp.einsum("gdp,gcp->gcd", s5_c_re, bb[0]) - jnp.einsum("gdp,gcp->gcd", s5_c_im, bb[1])
    cb_h = bd(parts(cb))
    tt = tiles["s5"]
    ridx = jnp.arange(b * tt)
    step = 2 * ((ridx % (b * tt // 2)) // b) + ridx // (b * tt // 2)
    perm = (ridx[None, :] == ((ridx % b) * tt + step)[:, None]).astype(BF16)
    y_s5, (wbr, wbs, wout, wup, wdown) = _s5(
        u, perm, b2_h.astype(BF16), abr.reshape(nh, 1, -1), abi.reshape(nh, 1, -1), c_h.astype(BF16),
        ca_h.astype(BF16), cb_h.astype(BF16), rowv(s5_d), s5_w_glu.astype(BF16), rowv(s5_b_glu), tt,
        tiles["s5_sub"], cast=[w_branch_rwkv, w_branch_s5, w_out, ffn_w_up, ffn_w_down])

    return _mix_ffn(x, y_rwkv, y_s5, gates, wbr, wbs, wout, rowv(norm_mix_post), rowv(norm_ffn_pre), wup, ffn_conv_w,
                    rowv(ffn_conv_b), wdown, rowv(norm_ffn_post), tiles["ffn"])


TILES = {"in": 512, "rwkv": 1024, "s5": 32, "s5_sub": 8, "ffn": 512}


def kernel(x, norm_mix_pre, norm_mix_post, norm_ffn_pre, norm_ffn_post, w_in, b_gate, rwkv_shift_mu, rwkv_w0, rwkv_w2, rwkv_a0, rwkv_a2, rwkv_g2, rwkv_k_k, rwkv_k_a, rwkv_r_k, rwkv_lnx_w, rwkv_lnx_b, s5_a_re, s5_a_im, s5_b_re, s5_b_im, s5_c_re, s5_c_im, s5_d, s5_log_step, s5_w_glu, s5_b_glu, w_branch_rwkv, w_branch_s5, w_out, ffn_w_up, ffn_conv_w, ffn_conv_b, ffn_w_down):
    depth = w_in.shape[0]
    for i in range(depth):
        x = _layer(x, norm_mix_pre[i], norm_mix_post[i], norm_ffn_pre[i], norm_ffn_post[i], w_in[i], b_gate[i],
                   rwkv_shift_mu[i], rwkv_w0[i], rwkv_w2[i], rwkv_a0[i], rwkv_a2[i], rwkv_g2[i], rwkv_k_k[i],
                   rwkv_k_a[i], rwkv_r_k[i].reshape(-1), rwkv_lnx_w[i], rwkv_lnx_b[i], s5_a_re[i], s5_a_im[i],
                   s5_b_re[i], s5_b_im[i], s5_c_re[i], s5_c_im[i], s5_d[i], s5_log_step[i], s5_w_glu[i],
                   s5_b_glu[i], w_branch_rwkv[i], w_branch_s5[i], w_out[i], ffn_w_up[i], ffn_conv_w[i],
                   ffn_conv_b[i], ffn_w_down[i], TILES)
    return x
```
